```python
import math
import jax, jax.numpy as jnp
from jax import lax
import numpy as np

D_MODEL = 1024
BATCH = 8
SEQ = 8192
DEPTH = 2
DEC_BATCH = 16
DEC_SEQ = 64
PAST_LEN = 4096

CHUNK = 64
N_EVEN = (DEPTH + 1) // 2
N_ODD = DEPTH // 2
EPS = 1e-6
GLA_HEADS = 4
GLA_DK = D_MODEL // 16
GLA_DV = D_MODEL // 8
GLA_RANK = 16
GLA_TAU = 16.0
GLA_BLOCK = 16
GLA_QK = GLA_HEADS * GLA_DK
GLA_V = GLA_HEADS * GLA_DV
GM_GROUPS = 4
GM_CH = D_MODEL // 8
GM_WIDTH = GM_GROUPS * GM_CH
GM_BLOCK = 128
IN0_WIDTH = 2 * GLA_QK + 2 * GLA_V + GLA_RANK + 2 * GM_WIDTH
MIX0_WIDTH = GLA_V + GM_WIDTH
SB_HEADS = 16
SB_DH = D_MODEL // SB_HEADS
SB_QBLOCK = 128
SB_KBLOCK = 128
N_GROUPS = 2
EXP_PER_GROUP = 4
N_EXPERTS = N_GROUPS * EXP_PER_GROUP
TOP_K = 2
D_EXPERT = D_MODEL // 4

kernel_name = "hybrid_streaming_encoder_step"


def rmsnorm(x, gain):
    xf = x.astype(jnp.float32)
    xf = xf * lax.rsqrt(jnp.mean(jnp.square(xf), axis=-1, keepdims=True) + EPS)
    return xf.astype(x.dtype) * gain


def layernorm(x, gain, bias):
    xf = x.astype(jnp.float32)
    mu = jnp.mean(xf, axis=-1, keepdims=True)
    xc = xf - mu
    xf = xc * lax.rsqrt(jnp.mean(jnp.square(xc), axis=-1, keepdims=True) + EPS)
    return xf.astype(x.dtype) * gain + bias


def modulation(c, w_mod, b_mod):
    m = jax.nn.silu(c) @ w_mod + b_mod
    return [t[:, None, :] for t in jnp.split(m, 6, axis=-1)]


def modulate(x, gain, shift, scale):
    return rmsnorm(x, gain) * (1 + scale) + shift


def gla_recurrence(q, k, v, log_a, s0):
    B, T, H, DK = q.shape
    DV = v.shape[-1]
    L = math.gcd(GLA_BLOCK, T)
    n = T // L

    def to_blocks(t):
        return jnp.moveaxis(t.reshape(B, n, L, H, t.shape[-1]).astype(jnp.float32), 1, 0)

    qc, kc, vc, ac = to_blocks(q), to_blocks(k), to_blocks(v), to_blocks(log_a)
    causal = jnp.tril(jnp.ones((L, L), bool))[None, :, :, None, None]

    def step(S, inp):
        qi, ki, vi, ai = inp
        b = jnp.cumsum(ai, axis=1)
        diff = jnp.where(causal, b[:, :, None] - b[:, None, :], -jnp.inf)
        attn = jnp.einsum('bihd,bjhd,bijhd->bhij', qi, ki, jnp.exp(diff))
        o = jnp.einsum('bhij,bjhv->bihv', attn, vi) + jnp.einsum('bihd,bhdv->bihv', qi * jnp.exp(b), S)
        b_last = b[:, -1]
        k_dec = ki * jnp.exp(b_last[:, None] - b)
        S_new = S * jnp.exp(b_last)[..., None] + jnp.einsum('bjhd,bjhv->bhdv', k_dec, vi)
        return S_new, o

    S_fin, o = lax.scan(step, s0.astype(jnp.float32), (qc, kc, vc, ac))
    return jnp.moveaxis(o, 0, 1).reshape(B, T, H, DV), S_fin


def spatial_gate(v, w_s, b_s):
    B, T, _ = v.shape
    L = min(GM_BLOCK, T)
    n = T // L
    w = jnp.where(jnp.tril(jnp.ones((GM_BLOCK, GM_BLOCK), bool)), w_s, 0.0)[:, :L, :L]
    vb = v.reshape(B, n, L, GM_GROUPS, GM_CH)
    z = jnp.einsum('gij,bnjgc->bnigc', w, vb) + b_s[:, :L].T[None, None, :, :, None]
    return z.reshape(B, T, GM_WIDTH)


def even_mixer(h, s0, w_in, w_a2, b_a, gla_norm, gm_ln_g, gm_ln_b, w_s, b_s, w_out):
    B, T, _ = h.shape
    o2 = 2 * GLA_QK
    o3 = o2 + GLA_V
    o4 = o3 + GLA_V
    o5 = o4 + GLA_RANK
    o6 = o5 + GM_WIDTH
    q, k, v, g, r, u, vg = jnp.split(h @ w_in, [GLA_QK, o2, o3, o4, o5, o6], axis=-1)
    q = q.reshape(B, T, GLA_HEADS, GLA_DK) * (GLA_DK ** -0.5)
    k = k.reshape(B, T, GLA_HEADS, GLA_DK)
    v = v.reshape(B, T, GLA_HEADS, GLA_DV)
    log_a = (jax.nn.log_sigmoid((r @ w_a2 + b_a).astype(jnp.float32)) / GLA_TAU).reshape(B, T, GLA_HEADS, GLA_DK)
    o, s_new = gla_recurrence(q, k, v, log_a, s0)
    o = rmsnorm(o, gla_norm.reshape(GLA_HEADS, GLA_DV)).astype(h.dtype).reshape(B, T, GLA_V)
    o = o * jax.nn.silu(g)
    u = jax.nn.gelu(u)
    vg = layernorm(jax.nn.gelu(vg), gm_ln_g, gm_ln_b)
    m = u * spatial_gate(vg, w_s, b_s)
    y = jnp.concatenate([o, m], axis=-1) @ w_out
    return y, s_new.astype(s0.dtype), vg


def qkv_proj(h, w_qkv):
    B, T, _ = h.shape
    q, k, v = jnp.split(h @ w_qkv, 3, axis=-1)
    shape = (B, T, SB_HEADS, SB_DH)
    return q.reshape(shape) * (SB_DH ** -0.5), k.reshape(shape), v.reshape(shape)


def reverse_cumsum(x):
    B, H, Q, S = x.shape
    nb = S // SB_KBLOCK
    idx = jnp.arange(SB_KBLOCK)
    upper = (idx[:, None] >= idx[None, :]).astype(x.dtype)
    within = jnp.einsum('bhqnj,js->bhqns', x.reshape(B, H, Q, nb, SB_KBLOCK), upper,
                        precision=lax.Precision.HIGHEST)
    bidx = jnp.arange(nb)
    after = (bidx[:, None] > bidx[None, :]).astype(x.dtype)
    carry = jnp.einsum('bhqm,mn->bhqn', within[..., 0], after, precision=lax.Precision.HIGHEST)
    return (within + carry[..., None]).reshape(B, H, Q, S)


def sb_attend(q, k, v, q_pos):
    S = k.shape[1]
    pad = (-S) % SB_KBLOCK
    if pad:
        k = jnp.pad(k, ((0, 0), (0, pad), (0, 0), (0, 0)))
        v = jnp.pad(v, ((0, 0), (0, pad), (0, 0), (0, 0)))
    z = jnp.einsum('bqhd,bshd->bhqs', q, k).astype(jnp.float32)
    visible = jnp.arange(S + pad)[None, :] < q_pos[:, None]
    log_1m = jnp.where(visible, jax.nn.log_sigmoid(-z), 0.0)
    r = reverse_cumsum(log_1m)
    a = jnp.exp(jnp.where(visible, z + r, -jnp.inf))
    return jnp.einsum('bhqs,bshd->bqhd', a.astype(v.dtype), v)


def sb_prompt(q, k, v):
    B, T, H, Dh = q.shape
    outs = []
    for i in range(T // SB_QBLOCK):
        lo, hi = i * SB_QBLOCK, (i + 1) * SB_QBLOCK
        pos = jnp.arange(lo, hi, dtype=jnp.int32)
        outs.append(sb_attend(q[:, lo:hi], k[:, :hi], v[:, :hi], pos))
    return jnp.concatenate(outs, axis=1).reshape(B, T, H * Dh)


def hier_moe(x, w_grp, b_grp, w_rt, b_rt, w1, w3, w2):
    B, T, _ = x.shape
    xf = x.astype(jnp.float32)
    p_grp = jax.nn.softmax(xf @ w_grp + b_grp, axis=-1)
    g_top, g_idx = lax.top_k(p_grp, 1)
    logits = (xf @ w_rt + b_rt).reshape(B, T, N_GROUPS, EXP_PER_GROUP)
    logits_sel = jnp.take_along_axis(logits, g_idx[..., None], axis=2)[:, :, 0]
    e_val, e_idx = lax.top_k(logits_sel, TOP_K)
    e_w = jax.nn.softmax(e_val, axis=-1) * g_top
    expert_id = g_idx * EXP_PER_GROUP + e_idx
    combine = jnp.sum(jax.nn.one_hot(expert_id, N_EXPERTS, dtype=jnp.float32) * e_w[..., None], axis=-2)
    combine = combine.astype(x.dtype)
    h1 = jnp.einsum('btd,edf->btef', x, w1)
    h3 = jnp.einsum('btd,edf->btef', x, w3)
    hid = jax.nn.silu(h1) * h3 * combine[..., None]
    return jnp.einsum('btef,efd->btd', hid, w2)


def setup_inputs(seed: int = 0) -> dict:
    key = jax.random.key(seed)
    ks = iter(jax.random.split(key, 40))

    def nrm(shape, scale):
        return jax.random.normal(next(ks), shape, jnp.float32) * scale

    D = D_MODEL
    return {
        "x_prompt": nrm((BATCH, SEQ, D), 1.0),
        "x_sample": nrm((DEC_BATCH, DEC_SEQ, D), 1.0),
        "c_prompt": nrm((BATCH, D), 1.0),
        "c_sample": nrm((DEC_BATCH, D), 1.0),
        "state_gla": nrm((N_EVEN, DEC_BATCH, GLA_HEADS, GLA_DK, GLA_DV), 0.5),
        "cache_k": nrm((N_ODD, DEC_BATCH, PAST_LEN, SB_HEADS, SB_DH), 1.0),
        "cache_v": nrm((N_ODD, DEC_BATCH, PAST_LEN, SB_HEADS, SB_DH), 1.0),
        "w_mod": nrm((DEPTH, D, 6 * D), 0.5 * D ** -0.5),
        "b_mod": nrm((DEPTH, 6 * D), 0.02),
        "g_mix_pre": 1.0 + nrm((DEPTH, D), 0.05),
        "g_mix_post": 1.0 + nrm((DEPTH, D), 0.05),
        "g_ffn_pre": 1.0 + nrm((DEPTH, D), 0.05),
        "g_ffn_post": 1.0 + nrm((DEPTH, D), 0.05),
        "w_in0": nrm((N_EVEN, D, IN0_WIDTH), D ** -0.5),
        "w_a2": nrm((N_EVEN, GLA_RANK, GLA_QK), GLA_RANK ** -0.5),
        "b_a": nrm((N_EVEN, GLA_QK), 0.1),
        "gla_norm": 1.0 + nrm((N_EVEN, GLA_V), 0.05),
        "gm_ln_g": 1.0 + nrm((N_EVEN, GM_WIDTH), 0.05),
        "gm_ln_b": nrm((N_EVEN, GM_WIDTH), 0.02),
        "w_s": nrm((N_EVEN, GM_GROUPS, GM_BLOCK, GM_BLOCK), GM_BLOCK ** -0.5),
        "b_s": nrm((N_EVEN, GM_GROUPS, GM_BLOCK), 0.02),
        "w_out0": nrm((N_EVEN, MIX0_WIDTH, D), MIX0_WIDTH ** -0.5),
        "w_qkv1": nrm((N_ODD, D, 3 * D), D ** -0.5),
        "w_out1": nrm((N_ODD, D, D), D ** -0.5),
        "w_grp": nrm((DEPTH, D, N_GROUPS), D ** -0.5),
        "b_grp": nrm((DEPTH, N_GROUPS), 0.01),
        "w_rt": nrm((DEPTH, D, N_EXPERTS), D ** -0.5),
        "b_rt": nrm((DEPTH, N_EXPERTS), 0.01),
        "w1": nrm((DEPTH, N_EXPERTS, D, D_EXPERT), D ** -0.5),
        "w3": nrm((DEPTH, N_EXPERTS, D, D_EXPERT), D ** -0.5),
        "w2": nrm((DEPTH, N_EXPERTS, D_EXPERT, D), D_EXPERT ** -0.5),
    }


def reference(x_prompt, x_sample, c_prompt, c_sample, state_gla, cache_k, cache_v,
              w_mod, b_mod, g_mix_pre, g_mix_post, g_ffn_pre, g_ffn_post,
              w_in0, w_a2, b_a, gla_norm, gm_ln_g, gm_ln_b, w_s, b_s, w_out0,
              w_qkv1, w_out1,
              w_grp, b_grp, w_rt, b_rt, w1, w3, w2):
    xp, xs = x_prompt, x_sample
    gla_p, gla_s, gmv_s = [], [], []
    kp_l, vp_l, ks_l, vs_l = [], [], [], []
    for layer in range(DEPTH):
        mp = modulation(c_prompt, w_mod[layer], b_mod[layer])
        ms = modulation(c_sample, w_mod[layer], b_mod[layer])
        hp = modulate(xp, g_mix_pre[layer], mp[0], mp[1])
        hs = modulate(xs, g_mix_pre[layer], ms[0], ms[1])
        i = layer // 2
        if layer % 2 == 0:
            ew = (w_in0[i], w_a2[i], b_a[i], gla_norm[i], gm_ln_g[i], gm_ln_b[i], w_s[i], b_s[i], w_out0[i])
            s0 = jnp.zeros((xp.shape[0], GLA_HEADS, GLA_DK, GLA_DV), state_gla.dtype)
            yp, sp, _ = even_mixer(hp, s0, *ew)
            ys, ss, vrows = even_mixer(hs, state_gla[i], *ew)
            gla_p.append(sp)
            gla_s.append(ss)
            gmv_s.append(vrows)
        else:
            qp, kp, vp = qkv_proj(hp, w_qkv1[i])
            yp = sb_prompt(qp, kp, vp) @ w_out1[i]
            qs, ks, vs = qkv_proj(hs, w_qkv1[i])
            k_all = jnp.concatenate([cache_k[i], ks], axis=1)
            v_all = jnp.concatenate([cache_v[i], vs], axis=1)
            pos_s = cache_k.shape[2] + jnp.arange(xs.shape[1], dtype=jnp.int32)
            ys = sb_attend(qs, k_all, v_all, pos_s).reshape(xs.shape[0], xs.shape[1], D_MODEL) @ w_out1[i]
            kp_l.append(kp)
            vp_l.append(vp)
            ks_l.append(ks)
            vs_l.append(vs)
        xp = xp + mp[2] * rmsnorm(yp, g_mix_post[layer])
        xs = xs + ms[2] * rmsnorm(ys, g_mix_post[layer])
        moe_w = (w_grp[layer], b_grp[layer], w_rt[layer], b_rt[layer], w1[layer], w3[layer], w2[layer])
        fp = hier_moe(modulate(xp, g_ffn_pre[layer], mp[3], mp[4]), *moe_w)
        fs = hier_moe(modulate(xs, g_ffn_pre[layer], ms[3], ms[4]), *moe_w)
        xp = xp + mp[5] * rmsnorm(fp, g_ffn_post[layer])
        xs = xs + ms[5] * rmsnorm(fs, g_ffn_post[layer])
    return (xp, xs, jnp.stack(gla_p), jnp.stack(gla_s), jnp.stack(gmv_s),
            jnp.stack(kp_l), jnp.stack(vp_l), jnp.stack(ks_l), jnp.stack(vs_l))
```

```python
import functools
import math

import numpy as np
import jax
import jax.numpy as jnp
from jax import lax
from jax.experimental import pallas as pl
from jax.experimental.pallas import tpu as pltpu

F32 = jnp.float32
BF16 = jnp.bfloat16

EPS = 1e-6
D_MODEL = 1024
GLA_HEADS = 4
GLA_DK = 64
GLA_DV = 128
GLA_RANK = 16
GLA_TAU = 16.0
GLA_QK = GLA_HEADS * GLA_DK
GLA_V = GLA_HEADS * GLA_DV
GM_GROUPS = 4
GM_CH = 128
GM_WIDTH = GM_GROUPS * GM_CH
GM_BLOCK = 128
SB_HEADS = 16
SB_DH = 64
N_GROUPS = 2
EXP_PER_GROUP = 4
N_EXPERTS = 8
D_EXPERT = 256

LANES = 128
RANK_PAD = LANES
IN0_PAD = 2 * GLA_QK + 2 * GLA_V + 2 * GM_WIDTH + RANK_PAD
ROUTER_PAD = LANES
SB_TK = 256
VMEM_LIMIT = 56 * 1024 * 1024


def _cparams(sem):
    return pltpu.CompilerParams(dimension_semantics=sem, vmem_limit_bytes=VMEM_LIMIT)


def _dot(a, b):
    return jnp.dot(a, b, preferred_element_type=F32)


def _dot_nt(a, b):
    return lax.dot_general(a, b, (((1,), (1,)), ((), ())), preferred_element_type=F32)


def _dot_tn(a, b):
    return lax.dot_general(a, b, (((0,), (0,)), ((), ())), preferred_element_type=F32)


def _split2(a):
    hi = a.astype(BF16)
    lo = (a - hi.astype(F32)).astype(BF16)
    return hi, lo


def _split3(a):
    hi = a.astype(BF16)
    r1 = a - hi.astype(F32)
    mid = r1.astype(BF16)
    lo = (r1 - mid.astype(F32)).astype(BF16)
    return hi, mid, lo


def _dot_x2(a, b_hi, b_lo):
    a_hi, a_lo = _split2(a)
    return _dot(a_hi, b_hi) + (_dot(a_lo, b_hi) + _dot(a_hi, b_lo))


def _sigmoid(x):
    return 1.0 / (1.0 + jnp.exp(-x))


def _silu(x):
    return x * _sigmoid(x)


def _softplus(x):
    return jnp.maximum(x, 0.0) + jnp.log(1.0 + jnp.exp(-jnp.abs(x)))


def _gelu_tanh(x):
    c = math.sqrt(2.0 / math.pi)
    return 0.5 * x * (1.0 + jnp.tanh(c * (x + 0.044715 * (x * x * x))))


def _rms(x, gain):
    ms = jnp.mean(x * x, axis=-1, keepdims=True)
    return x * lax.rsqrt(ms + EPS) * gain


def _modulate(x, gain, shift, scale):
    return _rms(x, gain) * (1.0 + scale) + shift


def _mod_kernel(c_ref, w_ref, b_ref, o_ref):
    a = _silu(c_ref[...])
    w_hi, w_lo = _split2(w_ref[0])
    o_ref[0] = _dot_x2(a, w_hi, w_lo) + b_ref[0]


def _modulation(c_all, w_mod, b_mod):
    depth, d, n = w_mod.shape
    rows = c_all.shape[0]
    tn = 1536
    return pl.pallas_call(
        _mod_kernel,
        grid=(depth, n // tn),
        in_specs=[
            pl.BlockSpec((rows, d), lambda l, j: (0, 0)),
            pl.BlockSpec((1, d, tn), lambda l, j: (l, 0, j)),
            pl.BlockSpec((1, 1, tn), lambda l, j: (l, 0, j)),
        ],
        out_specs=pl.BlockSpec((1, rows, tn), lambda l, j: (l, 0, j)),
        out_shape=jax.ShapeDtypeStruct((depth, rows, n), F32),
        compiler_params=_cparams(("arbitrary", "arbitrary")),
        name="modulation",
    )(c_all, w_mod, b_mod.reshape(depth, 1, n))


def _gla_level_matrix(tm):
    nlev = int(math.log2(tm))
    idx = np.arange(tm)
    blocks = []
    for lev in range(nlev):
        h = 1 << lev
        mid = (idx // (2 * h)) * (2 * h) + h
        right = idx >= mid
        m = np.zeros((tm, tm), np.float32)
        for i in range(tm):
            if right[i]:
                m[i, mid[i]:i + 1] = 1.0
            else:
                m[i, i + 1:mid[i]] = 1.0
        blocks.append(m)
    blocks.append(np.tril(np.ones((tm, tm), np.float32)))
    blocks.append(np.triu(np.ones((tm, tm), np.float32), 1))
    return np.concatenate(blocks, axis=0), nlev


def _l0_kernel(x_ref, mod_ref, s0_ref, gpre_ref, gpost_ref, win_ref, wa2h_ref, wa2l_ref, ba_ref,
               glan_ref, lng_ref, lnb_ref, ws_ref, bs_ref, wout_ref, mall_ref,
               x1_ref, snew_ref, *rest, tm, nlev, emit_vg):
    if emit_vg:
        vg_ref, s_scr = rest
    else:
        (s_scr,) = rest
    t = pl.program_id(1)

    @pl.when(t == 0)
    def _():
        s_scr[...] = s0_ref[0]

    x = x_ref[0]
    mod = mod_ref[0]
    h = _modulate(x, gpre_ref[...], mod[0:1], mod[1:2])
    p = _dot(h.astype(BF16), win_ref[...])
    o0 = 0
    q = p[:, o0:o0 + GLA_QK] * (GLA_DK ** -0.5)
    o0 += GLA_QK
    k = p[:, o0:o0 + GLA_QK]
    o0 += GLA_QK
    v = p[:, o0:o0 + GLA_V]
    o0 += GLA_V
    g = p[:, o0:o0 + GLA_V]
    o0 += GLA_V
    u = p[:, o0:o0 + GM_WIDTH]
    o0 += GM_WIDTH
    vgr = p[:, o0:o0 + GM_WIDTH]
    o0 += GM_WIDTH
    r = p[:, o0:o0 + RANK_PAD]

    xa = _dot_x2(r, wa2h_ref[...], wa2l_ref[...]) + ba_ref[...]
    la = -_softplus(-xa) * (1.0 / GLA_TAU)
    la3 = _split3(la)
    mall = mall_ref[...]
    e_all = _dot(mall, la3[0]) + (_dot(mall, la3[1]) + _dot(mall, la3[2]))

    row = lax.broadcasted_iota(jnp.int32, (tm, GLA_QK), 0)
    ii = lax.broadcasted_iota(jnp.int32, (tm, tm), 0)
    jj = lax.broadcasted_iota(jnp.int32, (tm, tm), 1)
    qb, kb = q.astype(BF16), k.astype(BF16)
    vb = v.astype(BF16)

    attn = [jnp.where(ii == jj, _dot_nt(qb[:, hh * GLA_DK:(hh + 1) * GLA_DK],
                                        kb[:, hh * GLA_DK:(hh + 1) * GLA_DK]), 0.0)
            for hh in range(GLA_HEADS)]
    for lev in range(nlev):
        w = jnp.exp(e_all[lev * tm:(lev + 1) * tm])
        is_right = ((row >> lev) & 1) == 1
        zz = (jnp.where(is_right, q, k) * w).astype(BF16)
        pair = ((ii >> lev) == (jj >> lev) + 1) & (((jj >> lev) & 1) == 0)
        for hh in range(GLA_HEADS):
            zh = zz[:, hh * GLA_DK:(hh + 1) * GLA_DK]
            attn[hh] = attn[hh] + jnp.where(pair, _dot_nt(zh, zh), 0.0)

    b_in = e_all[nlev * tm:(nlev + 1) * tm]
    b_suf = e_all[(nlev + 1) * tm:(nlev + 2) * tm]
    q_in = (q * jnp.exp(b_in)).astype(BF16)
    k_out = (k * jnp.exp(b_suf)).astype(BF16)
    ones = jnp.ones((tm, GLA_DV), BF16)
    dcol = jnp.exp(_dot_tn(la3[0], ones) + (_dot_tn(la3[1], ones) + _dot_tn(la3[2], ones)))

    glan = glan_ref[...]
    o_parts = []
    for hh in range(GLA_HEADS):
        ks = slice(hh * GLA_DK, (hh + 1) * GLA_DK)
        vs = slice(hh * GLA_DV, (hh + 1) * GLA_DV)
        s_h = s_scr[hh]
        o_h = _dot(attn[hh].astype(BF16), vb[:, vs]) + _dot(q_in[:, ks], s_h.astype(BF16))
        s_scr[hh] = dcol[ks] * s_h + _dot_tn(k_out[:, ks], vb[:, vs])
        o_parts.append(_rms(o_h, glan[:, vs]))
    o = jnp.concatenate(o_parts, axis=-1) * _silu(g)

    @pl.when(t == pl.num_programs(1) - 1)
    def _():
        snew_ref[0] = s_scr[...]

    u = _gelu_tanh(u)
    vg = _gelu_tanh(vgr)
    mu = jnp.mean(vg, axis=-1, keepdims=True)
    vc = vg - mu
    vg = vc * lax.rsqrt(jnp.mean(vc * vc, axis=-1, keepdims=True) + EPS) * lng_ref[...] + lnb_ref[...]
    if emit_vg:
        vg_ref[0] = vg
    vgb = vg.astype(BF16)
    bs = bs_ref[...]
    z_parts = []
    for gg in range(GM_GROUPS):
        cs = slice(gg * GM_CH, (gg + 1) * GM_CH)
        wm = jnp.where(ii >= jj, ws_ref[gg], 0.0).astype(BF16)
        z_parts.append(_dot(wm, vgb[:, cs]) + bs[:, gg:gg + 1])
    m = u * jnp.concatenate(z_parts, axis=-1)

    y = _dot(jnp.concatenate([o, m], axis=-1).astype(BF16), wout_ref[...])
    x1_ref[0] = x + mod[2:3] * _rms(y, gpost_ref[...])


def _layer0_mixer(x, mod, s0, g_pre, g_post, w_in, w_a2, b_a, gla_norm, ln_g, ln_b, w_s, b_s, w_out,
                  emit_vg):
    bsz, seq, d = x.shape
    tm = min(GM_BLOCK, seq)
    mall_np, nlev = _gla_level_matrix(tm)
    mall = jnp.asarray(mall_np, BF16)
    o2 = 2 * GLA_QK
    o3 = o2 + GLA_V
    o4 = o3 + GLA_V
    o5 = o4 + GLA_RANK
    o6 = o5 + GM_WIDTH
    w_in_p = jnp.concatenate(
        [w_in[:, :o4], w_in[:, o5:], w_in[:, o4:o5], jnp.zeros((d, RANK_PAD - GLA_RANK), w_in.dtype)],
        axis=1).astype(BF16)
    w_a2_p = jnp.concatenate([w_a2, jnp.zeros((RANK_PAD - GLA_RANK, GLA_QK), w_a2.dtype)], axis=0)
    wa2_hi = w_a2_p.astype(BF16)
    wa2_lo = (w_a2_p - wa2_hi.astype(F32)).astype(BF16)
    ws_t = w_s[:, :tm, :tm]
    bs_t = jnp.transpose(b_s[:, :tm])

    full = lambda shape: pl.BlockSpec(shape, lambda b, t: (0,) * len(shape))
    in_specs = [
        pl.BlockSpec((1, tm, d), lambda b, t: (b, t, 0)),
        pl.BlockSpec((1, 8, d), lambda b, t: (b, 0, 0)),
        pl.BlockSpec((1, GLA_HEADS, GLA_DK, GLA_DV), lambda b, t: (b, 0, 0, 0)),
        full((1, d)), full((1, d)),
        full((d, IN0_PAD)),
        full((RANK_PAD, GLA_QK)), full((RANK_PAD, GLA_QK)), full((1, GLA_QK)),
        full((1, GLA_V)), full((1, GM_WIDTH)), full((1, GM_WIDTH)),
        full((GM_GROUPS, tm, tm)), full((tm, GM_GROUPS)),
        full((GLA_V + GM_WIDTH, d)),
        full(((nlev + 2) * tm, tm)),
    ]
    out_specs = [
        pl.BlockSpec((1, tm, d), lambda b, t: (b, t, 0)),
        pl.BlockSpec((1, GLA_HEADS, GLA_DK, GLA_DV), lambda b, t: (b, 0, 0, 0)),
    ]
    out_shape = [
        jax.ShapeDtypeStruct((bsz, seq, d), F32),
        jax.ShapeDtypeStruct((bsz, GLA_HEADS, GLA_DK, GLA_DV), F32),
    ]
    if emit_vg:
        out_specs.append(pl.BlockSpec((1, tm, GM_WIDTH), lambda b, t: (b, t, 0)))
        out_shape.append(jax.ShapeDtypeStruct((bsz, seq, GM_WIDTH), F32))
    return pl.pallas_call(
        functools.partial(_l0_kernel, tm=tm, nlev=nlev, emit_vg=emit_vg),
        grid=(bsz, seq // tm),
        in_specs=in_specs,
        out_specs=out_specs,
        out_shape=out_shape,
        scratch_shapes=[pltpu.VMEM((GLA_HEADS, GLA_DK, GLA_DV), F32)],
        compiler_params=_cparams(("arbitrary", "arbitrary")),
        name="layer0_mixer",
    )(x, mod, s0, g_pre.reshape(1, d), g_post.reshape(1, d), w_in_p, wa2_hi, wa2_lo,
      b_a.reshape(1, GLA_QK), gla_norm.reshape(1, GLA_V), ln_g.reshape(1, GM_WIDTH),
      ln_b.reshape(1, GM_WIDTH), ws_t, bs_t, w_out.astype(BF16), mall)


def _route(logits):
    col = lambda c: logits[:, c:c + 1]
    lg0, lg1 = col(0), col(1)
    mx = jnp.maximum(lg0, lg1)
    e0, e1 = jnp.exp(lg0 - mx), jnp.exp(lg1 - mx)
    den = e0 + e1
    p0, p1 = e0 / den, e1 / den
    grp1 = p1 > p0
    g_top = jnp.where(grp1, p1, p0)
    s = [jnp.where(grp1, col(2 + EXP_PER_GROUP + j), col(2 + j)) for j in range(EXP_PER_GROUP)]
    v1 = jnp.maximum(jnp.maximum(s[0], s[1]), jnp.maximum(s[2], s[3]))
    first, taken = [], None
    for j in range(EXP_PER_GROUP):
        hit = s[j] == v1
        if taken is None:
            first.append(hit)
            taken = hit
        else:
            first.append(hit & jnp.logical_not(taken))
            taken = taken | hit
    neg = jnp.float32(-jnp.inf)
    s2 = [jnp.where(first[j], neg, s[j]) for j in range(EXP_PER_GROUP)]
    v2 = jnp.maximum(jnp.maximum(s2[0], s2[1]), jnp.maximum(s2[2], s2[3]))
    second, taken = [], None
    for j in range(EXP_PER_GROUP):
        hit = (s2[j] == v2) & jnp.logical_not(first[j])
        if taken is None:
            second.append(hit)
            taken = hit
        else:
            second.append(hit & jnp.logical_not(taken))
            taken = taken | hit
    e2 = jnp.exp(v2 - v1)
    w1 = g_top * (1.0 / (1.0 + e2))
    w2 = g_top * (e2 / (1.0 + e2))
    comb = []
    for e in range(N_EXPERTS):
        in_grp = grp1 if e >= EXP_PER_GROUP else jnp.logical_not(grp1)
        j = e % EXP_PER_GROUP
        cw = jnp.where(first[j], w1, 0.0) + jnp.where(second[j], w2, 0.0)
        comb.append(jnp.where(in_grp, cw, 0.0))
    return comb


def _moe_body(x, mod, gpre, gpost, wrh, wrl, br, w1, w3, w2):
    xm = _modulate(x, gpre, mod[3:4], mod[4:5])
    logits = _dot_x2(xm, wrh, wrl) + br
    comb = _route(logits)
    xb = xm.astype(BF16)
    h1 = _dot(xb, w1)
    h3 = _dot(xb, w3)
    hid = _silu(h1) * h3
    hid = jnp.concatenate(
        [hid[:, e * D_EXPERT:(e + 1) * D_EXPERT] * comb[e] for e in range(N_EXPERTS)], axis=-1)
    f = _dot(hid.astype(BF16), w2)
    return x + mod[5:6] * _rms(f, gpost)


def _moe_kernel(x_ref, mod_ref, gpre_ref, gpost_ref, wrh_ref, wrl_ref, br_ref, w1_ref, w3_ref, w2_ref,
                o_ref):
    o_ref[0] = _moe_body(x_ref[0], mod_ref[0], gpre_ref[...], gpost_ref[...], wrh_ref[...],
                         wrl_ref[...], br_ref[...], w1_ref[...], w3_ref[...], w2_ref[...])


def _moe_weights(w_grp, b_grp, w_rt, b_rt, w1, w3, w2):
    d = w_grp.shape[0]
    pad = ROUTER_PAD - N_GROUPS - N_EXPERTS
    wr = jnp.concatenate([w_grp, w_rt, jnp.zeros((d, pad), F32)], axis=1)
    wr_hi = wr.astype(BF16)
    wr_lo = (wr - wr_hi.astype(F32)).astype(BF16)
    br = jnp.concatenate([b_grp, b_rt, jnp.zeros((pad,), F32)]).reshape(1, ROUTER_PAD)
    w1c = jnp.transpose(w1, (1, 0, 2)).reshape(d, N_EXPERTS * D_EXPERT).astype(BF16)
    w3c = jnp.transpose(w3, (1, 0, 2)).reshape(d, N_EXPERTS * D_EXPERT).astype(BF16)
    w2c = w2.reshape(N_EXPERTS * D_EXPERT, d).astype(BF16)
    return wr_hi, wr_lo, br, w1c, w3c, w2c


def _moe_specs(d):
    full = lambda shape: pl.BlockSpec(shape, lambda b, t: (0,) * len(shape))
    hid = N_EXPERTS * D_EXPERT
    return [full((1, d)), full((1, d)), full((d, ROUTER_PAD)), full((d, ROUTER_PAD)),
            full((1, ROUTER_PAD)), full((d, hid)), full((d, hid)), full((hid, d))]


def _moe_layer(x, mod, g_pre, g_post, moe_w):
    bsz, seq, d = x.shape
    tm = min(256, seq)
    return pl.pallas_call(
        _moe_kernel,
        grid=(bsz, seq // tm),
        in_specs=[pl.BlockSpec((1, tm, d), lambda b, t: (b, t, 0)),
                  pl.BlockSpec((1, 8, d), lambda b, t: (b, 0, 0))] + _moe_specs(d),
        out_specs=pl.BlockSpec((1, tm, d), lambda b, t: (b, t, 0)),
        out_shape=jax.ShapeDtypeStruct((bsz, seq, d), F32),
        compiler_params=_cparams(("arbitrary", "arbitrary")),
        name="moe",
    )(x, mod, g_pre.reshape(1, d), g_post.reshape(1, d), *moe_w)


def _qkv_kernel(x_ref, mod_ref, gpre_ref, w_ref, k_ref, v_ref, qb_ref, kb_ref, vb_ref):
    d = x_ref.shape[-1]
    mod = mod_ref[0]
    h = _modulate(x_ref[0], gpre_ref[...], mod[0:1], mod[1:2])
    p = _dot(h.astype(BF16), w_ref[...])
    k = p[:, d:2 * d]
    v = p[:, 2 * d:3 * d]
    k_ref[0] = k
    v_ref[0] = v
    qb_ref[0] = (p[:, 0:d] * (SB_DH ** -0.5)).astype(BF16)
    kb_ref[0] = k.astype(BF16)
    vb_ref[0] = v.astype(BF16)


def _qkv_proj(x, mod, g_pre, w_qkv):
    bsz, seq, d = x.shape
    tm = min(256, seq)
    tile = pl.BlockSpec((1, tm, d), lambda b, t: (b, t, 0))
    return pl.pallas_call(
        _qkv_kernel,
        grid=(bsz, seq // tm),
        in_specs=[tile, pl.BlockSpec((1, 8, d), lambda b, t: (b, 0, 0)),
                  pl.BlockSpec((1, d), lambda b, t: (0, 0)),
                  pl.BlockSpec((d, 3 * d), lambda b, t: (0, 0))],
        out_specs=[tile] * 5,
        out_shape=[jax.ShapeDtypeStruct((bsz, seq, d), F32)] * 2
        + [jax.ShapeDtypeStruct((bsz, seq, d), BF16)] * 3,
        compiler_params=_cparams(("arbitrary", "arbitrary")),
        name="qkv_proj",
    )(x, mod, g_pre.reshape(1, d), w_qkv.astype(BF16))


def _suffix_matrix(n):
    u = np.triu(np.ones((n, n), np.float32)).T
    return jnp.asarray(np.concatenate([u, u], axis=0), BF16)


def _sb_tile(qa, qb, k, v, u2, ca, cb, vis):
    outs = []
    for qh, c in ((qa, ca), (qb, cb)):
        z = _dot_nt(qh, k)
        sp = _softplus(z)
        if vis is not None:
            sp = jnp.where(vis, sp, 0.0)
        hi, lo = _split2(sp)
        w = _dot(jnp.concatenate([hi, lo], axis=-1), u2)
        a = jnp.exp(z - w - c)
        if vis is not None:
            a = jnp.where(vis, a, 0.0)
        outs.append((_dot(a.astype(BF16), v), c + w[:, 0:1]))
    return outs[0][0], outs[1][0], outs[0][1], outs[1][1]


def _sb_prompt_kernel(q_ref, k_ref, v_ref, u2_ref, o_ref, *, tq):
    seq = q_ref.shape[1]
    u2 = u2_ref[...]
    lane = lax.broadcasted_iota(jnp.int32, (tq, LANES), 1)
    first = lane < SB_DH
    ti = lax.broadcasted_iota(jnp.int32, (tq, tq), 0)
    si = lax.broadcasted_iota(jnp.int32, (tq, tq), 1)
    vis = si < ti
    zero = jnp.zeros((tq, 1), F32)

    def qblock(qi, _):
        r0 = pl.multiple_of(qi * tq, tq)
        q = q_ref[0, pl.ds(r0, tq), :]
        qa = jnp.where(first, q, jnp.zeros_like(q))
        qb = jnp.where(first, jnp.zeros_like(q), q)
        oa, ob, ca, cb = _sb_tile(qa, qb, k_ref[0, pl.ds(r0, tq), :], v_ref[0, pl.ds(r0, tq), :],
                                  u2, zero, zero, vis)

        def kblock(n, carry):
            oa, ob, ca, cb = carry
            c0 = pl.multiple_of((qi - 1 - n) * tq, tq)
            da, db, ca, cb = _sb_tile(qa, qb, k_ref[0, pl.ds(c0, tq), :], v_ref[0, pl.ds(c0, tq), :],
                                      u2, ca, cb, None)
            return oa + da, ob + db, ca, cb

        oa, ob, ca, cb = lax.fori_loop(0, qi, kblock, (oa, ob, ca, cb))
        o_ref[0, pl.ds(r0, tq), :] = jnp.where(first, oa, ob).astype(o_ref.dtype)
        return 0

    lax.fori_loop(0, seq // tq, qblock, 0)


def _sb_prompt(qb, kb, vb):
    bsz, seq, d = qb.shape
    tq = min(SB_TK, seq)
    blk = pl.BlockSpec((1, seq, LANES), lambda b, h: (b, 0, h))
    return pl.pallas_call(
        functools.partial(_sb_prompt_kernel, tq=tq),
        grid=(bsz, d // LANES),
        in_specs=[blk, blk, blk, pl.BlockSpec((2 * tq, tq), lambda b, h: (0, 0))],
        out_specs=blk,
        out_shape=jax.ShapeDtypeStruct((bsz, seq, d), BF16),
        compiler_params=_cparams(("arbitrary", "arbitrary")),
        name="sb_prompt",
    )(qb, kb, vb, _suffix_matrix(tq))


def _sb_sample_kernel(q_ref, kn_ref, vn_ref, ck_ref, cv_ref, un_ref, uc_ref, o_ref, *, tk):
    tq = q_ref.shape[1]
    past = ck_ref.shape[1]
    lane = lax.broadcasted_iota(jnp.int32, (tq, LANES), 1)
    first = lane < SB_DH
    ti = lax.broadcasted_iota(jnp.int32, (tq, tq), 0)
    si = lax.broadcasted_iota(jnp.int32, (tq, tq), 1)
    zero = jnp.zeros((tq, 1), F32)
    q = q_ref[0]
    qa = jnp.where(first, q, jnp.zeros_like(q))
    qb = jnp.where(first, jnp.zeros_like(q), q)
    oa, ob, ca, cb = _sb_tile(qa, qb, kn_ref[0], vn_ref[0], un_ref[...], zero, zero, si < ti)
    uc = uc_ref[...]

    def kblock(n, carry):
        oa, ob, ca, cb = carry
        c0 = pl.multiple_of(past - (n + 1) * tk, tk)
        kk = ck_ref[0, pl.ds(c0, tk), :].astype(BF16)
        vv = cv_ref[0, pl.ds(c0, tk), :].astype(BF16)
        da, db, ca, cb = _sb_tile(qa, qb, kk, vv, uc, ca, cb, None)
        return oa + da, ob + db, ca, cb

    oa, ob, ca, cb = lax.fori_loop(0, past // tk, kblock, (oa, ob, ca, cb))
    o_ref[0] = jnp.where(first, oa, ob).astype(o_ref.dtype)


def _sb_sample(qb, kb, vb, cache_k, cache_v):
    bsz, seq, d = qb.shape
    past = cache_k.shape[1]
    tk = min(SB_TK, past)
    blk = pl.BlockSpec((1, seq, LANES), lambda b, h: (b, 0, h))
    cblk = pl.BlockSpec((1, past, LANES), lambda b, h: (b, 0, h))
    return pl.pallas_call(
        functools.partial(_sb_sample_kernel, tk=tk),
        grid=(bsz, d // LANES),
        in_specs=[blk, blk, blk, cblk, cblk,
                  pl.BlockSpec((2 * seq, seq), lambda b, h: (0, 0)),
                  pl.BlockSpec((2 * tk, tk), lambda b, h: (0, 0))],
        out_specs=blk,
        out_shape=jax.ShapeDtypeStruct((bsz, seq, d), BF16),
        compiler_params=_cparams(("arbitrary", "arbitrary")),
        name="sb_sample",
    )(qb, kb, vb, cache_k, cache_v, _suffix_matrix(seq), _suffix_matrix(tk))


def _outproj_kernel(a_ref, x_ref, mod_ref, gpost_ref, w_ref, o_ref):
    y = _dot(a_ref[0], w_ref[...])
    o_ref[0] = x_ref[0] + mod_ref[0][2:3] * _rms(y, gpost_ref[...])


def _out_proj(attn, x, mod, g_post, w_out):
    bsz, seq, d = x.shape
    tm = min(256, seq)
    tile = pl.BlockSpec((1, tm, d), lambda b, t: (b, t, 0))
    return pl.pallas_call(
        _outproj_kernel,
        grid=(bsz, seq // tm),
        in_specs=[tile, tile, pl.BlockSpec((1, 8, d), lambda b, t: (b, 0, 0)),
                  pl.BlockSpec((1, d), lambda b, t: (0, 0)),
                  pl.BlockSpec((d, d), lambda b, t: (0, 0))],
        out_specs=tile,
        out_shape=jax.ShapeDtypeStruct((bsz, seq, d), F32),
        compiler_params=_cparams(("arbitrary", "arbitrary")),
        name="out_proj",
    )(attn, x, mod, g_post.reshape(1, d), w_out.astype(BF16))


def kernel(x_prompt, x_sample, c_prompt, c_sample, state_gla, cache_k, cache_v, w_mod, b_mod, g_mix_pre, g_mix_post, g_ffn_pre, g_ffn_post, w_in0, w_a2, b_a, gla_norm, gm_ln_g, gm_ln_b, w_s, b_s, w_out0, w_qkv1, w_out1, w_grp, b_grp, w_rt, b_rt, w1, w3, w2):
    bp, dec_b = x_prompt.shape[0], x_sample.shape[0]
    d = x_prompt.shape[-1]
    depth = w_mod.shape[0]

    m = _modulation(jnp.concatenate([c_prompt, c_sample], axis=0), w_mod, b_mod)
    m = m.reshape(depth, bp + dec_b, 6, d)
    m = jnp.concatenate([m, jnp.zeros((depth, bp + dec_b, 2, d), F32)], axis=2)

    xp, xs = x_prompt, x_sample
    gla_p, gla_s, gmv_s = [], [], []
    kp_l, vp_l, ks_l, vs_l = [], [], [], []
    for layer in range(depth):
        mp, ms = m[layer, :bp], m[layer, bp:]
        i = layer // 2
        if layer % 2 == 0:
            ew = (w_in0[i], w_a2[i], b_a[i], gla_norm[i], gm_ln_g[i], gm_ln_b[i], w_s[i], b_s[i], w_out0[i])
            s0 = jnp.zeros((bp, GLA_HEADS, GLA_DK, GLA_DV), state_gla.dtype)
            xp, sp = _layer0_mixer(xp, mp, s0, g_mix_pre[layer], g_mix_post[layer], *ew, emit_vg=False)
            xs, ss, vrows = _layer0_mixer(xs, ms, state_gla[i], g_mix_pre[layer], g_mix_post[layer], *ew,
                                          emit_vg=True)
            gla_p.append(sp)
            gla_s.append(ss)
            gmv_s.append(vrows)
        else:
            kp, vp, qpb, kpb, vpb = _qkv_proj(xp, mp, g_mix_pre[layer], w_qkv1[i])
            ap = _sb_prompt(qpb, kpb, vpb)
            xp = _out_proj(ap, xp, mp, g_mix_post[layer], w_out1[i])
            ks, vs, qsb, ksb, vsb = _qkv_proj(xs, ms, g_mix_pre[layer], w_qkv1[i])
            past = cache_k.shape[2]
            a_s = _sb_sample(qsb, ksb, vsb, cache_k[i].reshape(dec_b, past, d),
                             cache_v[i].reshape(dec_b, past, d))
            xs = _out_proj(a_s, xs, ms, g_mix_post[layer], w_out1[i])
            hshape = (SB_HEADS, SB_DH)
            kp_l.append(kp.reshape(kp.shape[:2] + hshape))
            vp_l.append(vp.reshape(vp.shape[:2] + hshape))
            ks_l.append(ks.reshape(ks.shape[:2] + hshape))
            vs_l.append(vs.reshape(vs.shape[:2] + hshape))
        moe_w = _moe_weights(w_grp[layer], b_grp[layer], w_rt[layer], b_rt[layer], w1[layer], w3[layer], w2[layer])
        xp = _moe_layer(xp, mp, g_ffn_pre[layer], g_ffn_post[layer], moe_w)
        xs = _moe_layer(xs, ms, g_ffn_pre[layer], g_ffn_post[layer], moe_w)
    return (xp, xs, jnp.stack(gla_p), jnp.stack(gla_s), jnp.stack(gmv_s),
            jnp.stack(kp_l), jnp.stack(vp_l), jnp.stack(ks_l), jnp.stack(vs_l))
```

```python
import functools
import math

import numpy as np
import jax
import jax.numpy as jnp
from jax import lax
from jax.experimental import pallas as pl
from jax.experimental.pallas import tpu as pltpu

F32 = jnp.float32
BF16 = jnp.bfloat16

EPS = 1e-6
D_MODEL = 1024
GLA_HEADS = 4
GLA_DK = 64
GLA_DV = 128
GLA_RANK = 16
GLA_TAU = 16.0
GLA_QK = GLA_HEADS * GLA_DK
GLA_V = GLA_HEADS * GLA_DV
GM_GROUPS = 4
GM_CH = 128
GM_WIDTH = GM_GROUPS * GM_CH
GM_BLOCK = 128
SB_HEADS = 16
SB_DH = 64
N_GROUPS = 2
EXP_PER_GROUP = 4
N_EXPERTS = 8
D_EXPERT = 256

LANES = 128
RANK_PAD = LANES
IN0_PAD = 2 * GLA_QK + 2 * GLA_V + 2 * GM_WIDTH + RANK_PAD
ROUTER_PAD = LANES
SB_TK = 256
SB_EXIT = 104.0
VMEM_LIMIT = 56 * 1024 * 1024


def _cparams(sem):
    return pltpu.CompilerParams(dimension_semantics=sem, vmem_limit_bytes=VMEM_LIMIT)


def _dot(a, b):
    return jnp.dot(a, b, preferred_element_type=F32)


def _dot_nt(a, b):
    return lax.dot_general(a, b, (((1,), (1,)), ((), ())), preferred_element_type=F32)


def _dot_tn(a, b):
    return lax.dot_general(a, b, (((0,), (0,)), ((), ())), preferred_element_type=F32)


def _split2(a):
    hi = a.astype(BF16)
    lo = (a - hi.astype(F32)).astype(BF16)
    return hi, lo


def _split3(a):
    hi = a.astype(BF16)
    r1 = a - hi.astype(F32)
    mid = r1.astype(BF16)
    lo = (r1 - mid.astype(F32)).astype(BF16)
    return hi, mid, lo


def _dot_x2(a, b_hi, b_lo):
    a_hi, a_lo = _split2(a)
    return _dot(a_hi, b_hi) + (_dot(a_lo, b_hi) + _dot(a_hi, b_lo))


def _sigmoid(x):
    return 1.0 / (1.0 + jnp.exp(-x))


def _silu(x):
    return x * _sigmoid(x)


def _softplus(x):
    return jnp.maximum(x, 0.0) + jnp.log(1.0 + jnp.exp(-jnp.abs(x)))


def _gelu_tanh(x):
    c = math.sqrt(2.0 / math.pi)
    return 0.5 * x * (1.0 + jnp.tanh(c * (x + 0.044715 * (x * x * x))))


def _rms(x, gain):
    ms = jnp.mean(x * x, axis=-1, keepdims=True)
    return x * lax.rsqrt(ms + EPS) * gain


def _modulate(x, gain, shift, scale):
    return _rms(x, gain) * (1.0 + scale) + shift


def _mod_kernel(c_ref, w_ref, b_ref, o_ref):
    a = _silu(c_ref[...])
    w_hi, w_lo = _split2(w_ref[0])
    o_ref[0] = _dot_x2(a, w_hi, w_lo) + b_ref[0]


def _modulation(c_all, w_mod, b_mod):
    depth, d, n = w_mod.shape
    rows = c_all.shape[0]
    tn = 1536
    return pl.pallas_call(
        _mod_kernel,
        grid=(depth, n // tn),
        in_specs=[
            pl.BlockSpec((rows, d), lambda l, j: (0, 0)),
            pl.BlockSpec((1, d, tn), lambda l, j: (l, 0, j)),
            pl.BlockSpec((1, 1, tn), lambda l, j: (l, 0, j)),
        ],
        out_specs=pl.BlockSpec((1, rows, tn), lambda l, j: (l, 0, j)),
        out_shape=jax.ShapeDtypeStruct((depth, rows, n), F32),
        compiler_params=_cparams(("arbitrary", "arbitrary")),
        name="modulation",
    )(c_all, w_mod, b_mod.reshape(depth, 1, n))


def _gla_level_matrix(tm):
    nlev = int(math.log2(tm))
    idx = np.arange(tm)
    blocks = []
    for lev in range(nlev):
        h = 1 << lev
        mid = (idx // (2 * h)) * (2 * h) + h
        right = idx >= mid
        m = np.zeros((tm, tm), np.float32)
        for i in range(tm):
            if right[i]:
                m[i, mid[i]:i + 1] = 1.0
            else:
                m[i, i + 1:mid[i]] = 1.0
        blocks.append(m)
    blocks.append(np.tril(np.ones((tm, tm), np.float32)))
    blocks.append(np.triu(np.ones((tm, tm), np.float32), 1))
    return np.concatenate(blocks, axis=0), nlev


def _l0_kernel(x_ref, mod_ref, s0_ref, gpre_ref, gpost_ref, win_ref, wa2h_ref, wa2l_ref, ba_ref,
               glan_ref, lng_ref, lnb_ref, ws_ref, bs_ref, wout_ref, mall_ref,
               x1_ref, snew_ref, *rest, tm, nlev, emit_vg):
    if emit_vg:
        vg_ref, s_scr = rest
    else:
        (s_scr,) = rest
    t = pl.program_id(1)

    @pl.when(t == 0)
    def _():
        s_scr[...] = s0_ref[0]

    x = x_ref[0]
    mod = mod_ref[0]
    h = _modulate(x, gpre_ref[...], mod[0:1], mod[1:2])
    p = _dot(h.astype(BF16), win_ref[...])
    o0 = 0
    q = p[:, o0:o0 + GLA_QK] * (GLA_DK ** -0.5)
    o0 += GLA_QK
    k = p[:, o0:o0 + GLA_QK]
    o0 += GLA_QK
    v = p[:, o0:o0 + GLA_V]
    o0 += GLA_V
    g = p[:, o0:o0 + GLA_V]
    o0 += GLA_V
    u = p[:, o0:o0 + GM_WIDTH]
    o0 += GM_WIDTH
    vgr = p[:, o0:o0 + GM_WIDTH]
    o0 += GM_WIDTH
    r = p[:, o0:o0 + RANK_PAD]

    xa = _dot_x2(r, wa2h_ref[...], wa2l_ref[...]) + ba_ref[...]
    la = -_softplus(-xa) * (1.0 / GLA_TAU)
    la3 = _split3(la)
    mall = mall_ref[...]
    e_all = _dot(mall, la3[0]) + (_dot(mall, la3[1]) + _dot(mall, la3[2]))

    row = lax.broadcasted_iota(jnp.int32, (tm, GLA_QK), 0)
    ii = lax.broadcasted_iota(jnp.int32, (tm, tm), 0)
    jj = lax.broadcasted_iota(jnp.int32, (tm, tm), 1)
    qb, kb = q.astype(BF16), k.astype(BF16)
    vb = v.astype(BF16)

    attn = [jnp.where(ii == jj, _dot_nt(qb[:, hh * GLA_DK:(hh + 1) * GLA_DK],
                                        kb[:, hh * GLA_DK:(hh + 1) * GLA_DK]), 0.0)
            for hh in range(GLA_HEADS)]
    for lev in range(nlev):
        w = jnp.exp(e_all[lev * tm:(lev + 1) * tm])
        is_right = ((row >> lev) & 1) == 1
        zz = (jnp.where(is_right, q, k) * w).astype(BF16)
        pair = ((ii >> lev) == (jj >> lev) + 1) & (((jj >> lev) & 1) == 0)
        for hh in range(GLA_HEADS):
            zh = zz[:, hh * GLA_DK:(hh + 1) * GLA_DK]
            attn[hh] = attn[hh] + jnp.where(pair, _dot_nt(zh, zh), 0.0)

    b_in = e_all[nlev * tm:(nlev + 1) * tm]
    b_suf = e_all[(nlev + 1) * tm:(nlev + 2) * tm]
    q_in = (q * jnp.exp(b_in)).astype(BF16)
    k_out = (k * jnp.exp(b_suf)).astype(BF16)
    ones = jnp.ones((tm, GLA_DV), BF16)
    dcol = jnp.exp(_dot_tn(la3[0], ones) + (_dot_tn(la3[1], ones) + _dot_tn(la3[2], ones)))

    glan = glan_ref[...]
    o_parts = []
    for hh in range(GLA_HEADS):
        ks = slice(hh * GLA_DK, (hh + 1) * GLA_DK)
        vs = slice(hh * GLA_DV, (hh + 1) * GLA_DV)
        s_h = s_scr[hh]
        o_h = _dot(attn[hh].astype(BF16), vb[:, vs]) + _dot(q_in[:, ks], s_h.astype(BF16))
        s_scr[hh] = dcol[ks] * s_h + _dot_tn(k_out[:, ks], vb[:, vs])
        o_parts.append(_rms(o_h, glan[:, vs]))
    o = jnp.concatenate(o_parts, axis=-1) * _silu(g)

    @pl.when(t == pl.num_programs(1) - 1)
    def _():
        snew_ref[0] = s_scr[...]

    u = _gelu_tanh(u)
    vg = _gelu_tanh(vgr)
    mu = jnp.mean(vg, axis=-1, keepdims=True)
    vc = vg - mu
    vg = vc * lax.rsqrt(jnp.mean(vc * vc, axis=-1, keepdims=True) + EPS) * lng_ref[...] + lnb_ref[...]
    if emit_vg:
        vg_ref[0] = vg
    vgb = vg.astype(BF16)
    bs = bs_ref[...]
    z_parts = []
    for gg in range(GM_GROUPS):
        cs = slice(gg * GM_CH, (gg + 1) * GM_CH)
        wm = jnp.where(ii >= jj, ws_ref[gg], 0.0).astype(BF16)
        z_parts.append(_dot(wm, vgb[:, cs]) + bs[:, gg:gg + 1])
    m = u * jnp.concatenate(z_parts, axis=-1)

    y = _dot(jnp.concatenate([o, m], axis=-1).astype(BF16), wout_ref[...])
    x1_ref[0] = x + mod[2:3] * _rms(y, gpost_ref[...])


def _layer0_mixer(x, mod, s0, g_pre, g_post, w_in, w_a2, b_a, gla_norm, ln_g, ln_b, w_s, b_s, w_out,
                  emit_vg):
    bsz, seq, d = x.shape
    tm = min(GM_BLOCK, seq)
    mall_np, nlev = _gla_level_matrix(tm)
    mall = jnp.asarray(mall_np, BF16)
    o2 = 2 * GLA_QK
    o3 = o2 + GLA_V
    o4 = o3 + GLA_V
    o5 = o4 + GLA_RANK
    o6 = o5 + GM_WIDTH
    w_in_p = jnp.concatenate(
        [w_in[:, :o4], w_in[:, o5:], w_in[:, o4:o5], jnp.zeros((d, RANK_PAD - GLA_RANK), w_in.dtype)],
        axis=1).astype(BF16)
    w_a2_p = jnp.concatenate([w_a2, jnp.zeros((RANK_PAD - GLA_RANK, GLA_QK), w_a2.dtype)], axis=0)
    wa2_hi = w_a2_p.astype(BF16)
    wa2_lo = (w_a2_p - wa2_hi.astype(F32)).astype(BF16)
    ws_t = w_s[:, :tm, :tm]
    bs_t = jnp.transpose(b_s[:, :tm])

    full = lambda shape: pl.BlockSpec(shape, lambda b, t: (0,) * len(shape))
    in_specs = [
        pl.BlockSpec((1, tm, d), lambda b, t: (b, t, 0)),
        pl.BlockSpec((1, 8, d), lambda b, t: (b, 0, 0)),
        pl.BlockSpec((1, GLA_HEADS, GLA_DK, GLA_DV), lambda b, t: (b, 0, 0, 0)),
        full((1, d)), full((1, d)),
        full((d, IN0_PAD)),
        full((RANK_PAD, GLA_QK)), full((RANK_PAD, GLA_QK)), full((1, GLA_QK)),
        full((1, GLA_V)), full((1, GM_WIDTH)), full((1, GM_WIDTH)),
        full((GM_GROUPS, tm, tm)), full((tm, GM_GROUPS)),
        full((GLA_V + GM_WIDTH, d)),
        full(((nlev + 2) * tm, tm)),
    ]
    out_specs = [
        pl.BlockSpec((1, tm, d), lambda b, t: (b, t, 0)),
        pl.BlockSpec((1, GLA_HEADS, GLA_DK, GLA_DV), lambda b, t: (b, 0, 0, 0)),
    ]
    out_shape = [
        jax.ShapeDtypeStruct((bsz, seq, d), F32),
        jax.ShapeDtypeStruct((bsz, GLA_HEADS, GLA_DK, GLA_DV), F32),
    ]
    if emit_vg:
        out_specs.append(pl.BlockSpec((1, tm, GM_WIDTH), lambda b, t: (b, t, 0)))
        out_shape.append(jax.ShapeDtypeStruct((bsz, seq, GM_WIDTH), F32))
    return pl.pallas_call(
        functools.partial(_l0_kernel, tm=tm, nlev=nlev, emit_vg=emit_vg),
        grid=(bsz, seq // tm),
        in_specs=in_specs,
        out_specs=out_specs,
        out_shape=out_shape,
        scratch_shapes=[pltpu.VMEM((GLA_HEADS, GLA_DK, GLA_DV), F32)],
        compiler_params=_cparams(("arbitrary", "arbitrary")),
        name="layer0_mixer",
    )(x, mod, s0, g_pre.reshape(1, d), g_post.reshape(1, d), w_in_p, wa2_hi, wa2_lo,
      b_a.reshape(1, GLA_QK), gla_norm.reshape(1, GLA_V), ln_g.reshape(1, GM_WIDTH),
      ln_b.reshape(1, GM_WIDTH), ws_t, bs_t, w_out.astype(BF16), mall)


def _route(logits):
    col = lambda c: logits[:, c:c + 1]
    lg0, lg1 = col(0), col(1)
    mx = jnp.maximum(lg0, lg1)
    e0, e1 = jnp.exp(lg0 - mx), jnp.exp(lg1 - mx)
    den = e0 + e1
    p0, p1 = e0 / den, e1 / den
    grp1 = p1 > p0
    g_top = jnp.where(grp1, p1, p0)
    s = [jnp.where(grp1, col(2 + EXP_PER_GROUP + j), col(2 + j)) for j in range(EXP_PER_GROUP)]
    v1 = jnp.maximum(jnp.maximum(s[0], s[1]), jnp.maximum(s[2], s[3]))
    first, taken = [], None
    for j in range(EXP_PER_GROUP):
        hit = s[j] == v1
        if taken is None:
            first.append(hit)
            taken = hit
        else:
            first.append(hit & jnp.logical_not(taken))
            taken = taken | hit
    neg = jnp.float32(-jnp.inf)
    s2 = [jnp.where(first[j], neg, s[j]) for j in range(EXP_PER_GROUP)]
    v2 = jnp.maximum(jnp.maximum(s2[0], s2[1]), jnp.maximum(s2[2], s2[3]))
    second, taken = [], None
    for j in range(EXP_PER_GROUP):
        hit = (s2[j] == v2) & jnp.logical_not(first[j])
        if taken is None:
            second.append(hit)
            taken = hit
        else:
            second.append(hit & jnp.logical_not(taken))
            taken = taken | hit
    e2 = jnp.exp(v2 - v1)
    w1 = g_top * (1.0 / (1.0 + e2))
    w2 = g_top * (e2 / (1.0 + e2))
    comb = []
    for e in range(N_EXPERTS):
        in_grp = grp1 if e >= EXP_PER_GROUP else jnp.logical_not(grp1)
        j = e % EXP_PER_GROUP
        cw = jnp.where(first[j], w1, 0.0) + jnp.where(second[j], w2, 0.0)
        comb.append(jnp.where(in_grp, cw, 0.0))
    return comb


def _moe_body(x, mod, gpre, gpost, wrh, wrl, br, w1, w3, w2):
    xm = _modulate(x, gpre, mod[3:4], mod[4:5])
    logits = _dot_x2(xm, wrh, wrl) + br
    comb = _route(logits)
    xb = xm.astype(BF16)
    h1 = _dot(xb, w1)
    h3 = _dot(xb, w3)
    hid = _silu(h1) * h3
    hid = jnp.concatenate(
        [hid[:, e * D_EXPERT:(e + 1) * D_EXPERT] * comb[e] for e in range(N_EXPERTS)], axis=-1)
    f = _dot(hid.astype(BF16), w2)
    return x + mod[5:6] * _rms(f, gpost)


def _moe_kernel(x_ref, mod_ref, gpre_ref, gpost_ref, wrh_ref, wrl_ref, br_ref, w1_ref, w3_ref, w2_ref,
                o_ref):
    o_ref[0] = _moe_body(x_ref[0], mod_ref[0], gpre_ref[...], gpost_ref[...], wrh_ref[...],
                         wrl_ref[...], br_ref[...], w1_ref[...], w3_ref[...], w2_ref[...])


def _moe_weights(w_grp, b_grp, w_rt, b_rt, w1, w3, w2):
    d = w_grp.shape[0]
    pad = ROUTER_PAD - N_GROUPS - N_EXPERTS
    wr = jnp.concatenate([w_grp, w_rt, jnp.zeros((d, pad), F32)], axis=1)
    wr_hi = wr.astype(BF16)
    wr_lo = (wr - wr_hi.astype(F32)).astype(BF16)
    br = jnp.concatenate([b_grp, b_rt, jnp.zeros((pad,), F32)]).reshape(1, ROUTER_PAD)
    w1c = jnp.transpose(w1, (1, 0, 2)).reshape(d, N_EXPERTS * D_EXPERT).astype(BF16)
    w3c = jnp.transpose(w3, (1, 0, 2)).reshape(d, N_EXPERTS * D_EXPERT).astype(BF16)
    w2c = w2.reshape(N_EXPERTS * D_EXPERT, d).astype(BF16)
    return wr_hi, wr_lo, br, w1c, w3c, w2c


def _moe_specs(d):
    full = lambda shape: pl.BlockSpec(shape, lambda b, t: (0,) * len(shape))
    hid = N_EXPERTS * D_EXPERT
    return [full((1, d)), full((1, d)), full((d, ROUTER_PAD)), full((d, ROUTER_PAD)),
            full((1, ROUTER_PAD)), full((d, hid)), full((d, hid)), full((hid, d))]


def _moe_layer(x, mod, g_pre, g_post, moe_w):
    bsz, seq, d = x.shape
    tm = min(256, seq)
    return pl.pallas_call(
        _moe_kernel,
        grid=(bsz, seq // tm),
        in_specs=[pl.BlockSpec((1, tm, d), lambda b, t: (b, t, 0)),
                  pl.BlockSpec((1, 8, d), lambda b, t: (b, 0, 0))] + _moe_specs(d),
        out_specs=pl.BlockSpec((1, tm, d), lambda b, t: (b, t, 0)),
        out_shape=jax.ShapeDtypeStruct((bsz, seq, d), F32),
        compiler_params=_cparams(("arbitrary", "arbitrary")),
        name="moe",
    )(x, mod, g_pre.reshape(1, d), g_post.reshape(1, d), *moe_w)


def _qkv_kernel(x_ref, mod_ref, gpre_ref, w_ref, k_ref, v_ref, qb_ref, kb_ref, vb_ref):
    d = x_ref.shape[-1]
    mod = mod_ref[0]
    h = _modulate(x_ref[0], gpre_ref[...], mod[0:1], mod[1:2])
    p = _dot(h.astype(BF16), w_ref[...])
    k = p[:, d:2 * d]
    v = p[:, 2 * d:3 * d]
    k_ref[0] = k
    v_ref[0] = v
    qb_ref[0] = (p[:, 0:d] * (SB_DH ** -0.5)).astype(BF16)
    kb_ref[0] = k.astype(BF16)
    vb_ref[0] = v.astype(BF16)


def _qkv_proj(x, mod, g_pre, w_qkv):
    bsz, seq, d = x.shape
    tm = min(256, seq)
    tile = pl.BlockSpec((1, tm, d), lambda b, t: (b, t, 0))
    return pl.pallas_call(
        _qkv_kernel,
        grid=(bsz, seq // tm),
        in_specs=[tile, pl.BlockSpec((1, 8, d), lambda b, t: (b, 0, 0)),
                  pl.BlockSpec((1, d), lambda b, t: (0, 0)),
                  pl.BlockSpec((d, 3 * d), lambda b, t: (0, 0))],
        out_specs=[tile] * 5,
        out_shape=[jax.ShapeDtypeStruct((bsz, seq, d), F32)] * 2
        + [jax.ShapeDtypeStruct((bsz, seq, d), BF16)] * 3,
        compiler_params=_cparams(("arbitrary", "arbitrary")),
        name="qkv_proj",
    )(x, mod, g_pre.reshape(1, d), w_qkv.astype(BF16))


def _suffix_matrix(n):
    u = np.triu(np.ones((n, n), np.float32)).T
    return jnp.asarray(np.concatenate([u, u], axis=0), BF16)


def _sb_tile(qa, qb, k, v, u2, ca, cb, vis):
    outs = []
    for qh, c in ((qa, ca), (qb, cb)):
        z = _dot_nt(qh, k)
        sp = _softplus(z)
        if vis is not None:
            sp = jnp.where(vis, sp, 0.0)
        hi, lo = _split2(sp)
        w = _dot(jnp.concatenate([hi, lo], axis=-1), u2)
        a = jnp.exp(z - w - c)
        if vis is not None:
            a = jnp.where(vis, a, 0.0)
        outs.append((_dot(a.astype(BF16), v), c + w[:, 0:1]))
    return outs[0][0], outs[1][0], outs[0][1], outs[1][1]


def _sb_rest(tile_at, n_rest, oa, ob, ca, cb):
    def low(ca, cb):
        return jnp.min(jnp.minimum(ca, cb))

    def cond(c):
        return (c[0] < n_rest) & (c[1] <= SB_EXIT)

    def body(c):
        n, _, oa, ob, ca, cb = c
        da, db, ca, cb = tile_at(n, ca, cb)
        return n + 1, low(ca, cb), oa + da, ob + db, ca, cb

    c = lax.while_loop(cond, body, (jnp.int32(0), low(ca, cb), oa, ob, ca, cb))
    return c[2], c[3]


def _sb_prompt_kernel(q_ref, k_ref, v_ref, u2_ref, o_ref, *, tq):
    seq = q_ref.shape[1]
    u2 = u2_ref[...]
    lane = lax.broadcasted_iota(jnp.int32, (tq, LANES), 1)
    first = lane < SB_DH
    ti = lax.broadcasted_iota(jnp.int32, (tq, tq), 0)
    si = lax.broadcasted_iota(jnp.int32, (tq, tq), 1)
    vis = si < ti
    zero = jnp.zeros((tq, 1), F32)

    def rows(ref, blk):
        return ref[0, pl.ds(pl.multiple_of(blk * tq, tq), tq), :]

    def split_heads(qi):
        q = rows(q_ref, qi)
        return jnp.where(first, q, jnp.zeros_like(q)), jnp.where(first, jnp.zeros_like(q), q)

    def store(qi, oa, ob):
        o_ref[0, pl.ds(pl.multiple_of(qi * tq, tq), tq), :] = jnp.where(first, oa, ob).astype(o_ref.dtype)

    qa, qb = split_heads(0)
    oa, ob, _, _ = _sb_tile(qa, qb, rows(k_ref, 0), rows(v_ref, 0), u2, zero, zero, vis)
    store(0, oa, ob)

    def qblock(qi, _):
        qa, qb = split_heads(qi)
        oa, ob, ca, cb = _sb_tile(qa, qb, rows(k_ref, qi), rows(v_ref, qi), u2, zero, zero, vis)
        da, db, ca, cb = _sb_tile(qa, qb, rows(k_ref, qi - 1), rows(v_ref, qi - 1), u2, ca, cb, None)

        def tile_at(n, ca, cb):
            return _sb_tile(qa, qb, rows(k_ref, qi - 2 - n), rows(v_ref, qi - 2 - n), u2, ca, cb, None)

        oa, ob = _sb_rest(tile_at, qi - 1, oa + da, ob + db, ca, cb)
        store(qi, oa, ob)
        return 0

    lax.fori_loop(1, seq // tq, qblock, 0)


def _sb_prompt(qb, kb, vb):
    bsz, seq, d = qb.shape
    tq = min(SB_TK, seq)
    blk = pl.BlockSpec((1, seq, LANES), lambda b, h: (b, 0, h))
    return pl.pallas_call(
        functools.partial(_sb_prompt_kernel, tq=tq),
        grid=(bsz, d // LANES),
        in_specs=[blk, blk, blk, pl.BlockSpec((2 * tq, tq), lambda b, h: (0, 0))],
        out_specs=blk,
        out_shape=jax.ShapeDtypeStruct((bsz, seq, d), BF16),
        compiler_params=_cparams(("arbitrary", "arbitrary")),
        name="sb_prompt",
    )(qb, kb, vb, _suffix_matrix(tq))


def _sb_sample_kernel(q_ref, kn_ref, vn_ref, ck_ref, cv_ref, un_ref, uc_ref, o_ref, *, tk):
    tq = q_ref.shape[1]
    nblk = ck_ref.shape[1] // tk
    lane = lax.broadcasted_iota(jnp.int32, (tq, LANES), 1)
    first = lane < SB_DH
    ti = lax.broadcasted_iota(jnp.int32, (tq, tq), 0)
    si = lax.broadcasted_iota(jnp.int32, (tq, tq), 1)
    zero = jnp.zeros((tq, 1), F32)
    q = q_ref[0]
    qa = jnp.where(first, q, jnp.zeros_like(q))
    qb = jnp.where(first, jnp.zeros_like(q), q)
    uc = uc_ref[...]

    def cache_tile(blk, ca, cb):
        c0 = pl.multiple_of(blk * tk, tk)
        kk = ck_ref[0, pl.ds(c0, tk), :].astype(BF16)
        vv = cv_ref[0, pl.ds(c0, tk), :].astype(BF16)
        return _sb_tile(qa, qb, kk, vv, uc, ca, cb, None)

    oa, ob, ca, cb = _sb_tile(qa, qb, kn_ref[0], vn_ref[0], un_ref[...], zero, zero, si < ti)
    da, db, ca, cb = cache_tile(nblk - 1, ca, cb)
    oa, ob = _sb_rest(lambda n, ca, cb: cache_tile(nblk - 2 - n, ca, cb), nblk - 1,
                      oa + da, ob + db, ca, cb)
    o_ref[0] = jnp.where(first, oa, ob).astype(o_ref.dtype)


def _sb_sample(qb, kb, vb, cache_k, cache_v):
    bsz, seq, d = qb.shape
    past = cache_k.shape[1]
    tk = min(SB_TK, past)
    blk = pl.BlockSpec((1, seq, LANES), lambda b, h: (b, 0, h))
    cblk = pl.BlockSpec((1, past, LANES), lambda b, h: (b, 0, h))
    return pl.pallas_call(
        functools.partial(_sb_sample_kernel, tk=tk),
        grid=(bsz, d // LANES),
        in_specs=[blk, blk, blk, cblk, cblk,
                  pl.BlockSpec((2 * seq, seq), lambda b, h: (0, 0)),
                  pl.BlockSpec((2 * tk, tk), lambda b, h: (0, 0))],
        out_specs=blk,
        out_shape=jax.ShapeDtypeStruct((bsz, seq, d), BF16),
        compiler_params=_cparams(("arbitrary", "arbitrary")),
        name="sb_sample",
    )(qb, kb, vb, cache_k, cache_v, _suffix_matrix(seq), _suffix_matrix(tk))


def _outproj_kernel(a_ref, x_ref, mod_ref, gpost_ref, w_ref, o_ref):
    y = _dot(a_ref[0], w_ref[...])
    o_ref[0] = x_ref[0] + mod_ref[0][2:3] * _rms(y, gpost_ref[...])


def _out_proj(attn, x, mod, g_post, w_out):
    bsz, seq, d = x.shape
    tm = min(256, seq)
    tile = pl.BlockSpec((1, tm, d), lambda b, t: (b, t, 0))
    return pl.pallas_call(
        _outproj_kernel,
        grid=(bsz, seq // tm),
        in_specs=[tile, tile, pl.BlockSpec((1, 8, d), lambda b, t: (b, 0, 0)),
                  pl.BlockSpec((1, d), lambda b, t: (0, 0)),
                  pl.BlockSpec((d, d), lambda b, t: (0, 0))],
        out_specs=tile,
        out_shape=jax.ShapeDtypeStruct((bsz, seq, d), F32),
        compiler_params=_cparams(("arbitrary", "arbitrary")),
        name="out_proj",
    )(attn, x, mod, g_post.reshape(1, d), w_out.astype(BF16))


def kernel(x_prompt, x_sample, c_prompt, c_sample, state_gla, cache_k, cache_v, w_mod, b_mod, g_mix_pre, g_mix_post, g_ffn_pre, g_ffn_post, w_in0, w_a2, b_a, gla_norm, gm_ln_g, gm_ln_b, w_s, b_s, w_out0, w_qkv1, w_out1, w_grp, b_grp, w_rt, b_rt, w1, w3, w2):
    bp, dec_b = x_prompt.shape[0], x_sample.shape[0]
    d = x_prompt.shape[-1]
    depth = w_mod.shape[0]

    m = _modulation(jnp.concatenate([c_prompt, c_sample], axis=0), w_mod, b_mod)
    m = m.reshape(depth, bp + dec_b, 6, d)
    m = jnp.concatenate([m, jnp.zeros((depth, bp + dec_b, 2, d), F32)], axis=2)

    xp, xs = x_prompt, x_sample
    gla_p, gla_s, gmv_s = [], [], []
    kp_l, vp_l, ks_l, vs_l = [], [], [], []
    for layer in range(depth):
        mp, ms = m[layer, :bp], m[layer, bp:]
        i = layer // 2
        if layer % 2 == 0:
            ew = (w_in0[i], w_a2[i], b_a[i], gla_norm[i], gm_ln_g[i], gm_ln_b[i], w_s[i], b_s[i], w_out0[i])
            s0 = jnp.zeros((bp, GLA_HEADS, GLA_DK, GLA_DV), state_gla.dtype)
            xp, sp = _layer0_mixer(xp, mp, s0, g_mix_pre[layer], g_mix_post[layer], *ew, emit_vg=False)
            xs, ss, vrows = _layer0_mixer(xs, ms, state_gla[i], g_mix_pre[layer], g_mix_post[layer], *ew,
                                          emit_vg=True)
            gla_p.append(sp)
            gla_s.append(ss)
            gmv_s.append(vrows)
        else:
            kp, vp, qpb, kpb, vpb = _qkv_proj(xp, mp, g_mix_pre[layer], w_qkv1[i])
            ap = _sb_prompt(qpb, kpb, vpb)
            xp = _out_proj(ap, xp, mp, g_mix_post[layer], w_out1[i])
            ks, vs, qsb, ksb, vsb = _qkv_proj(xs, ms, g_mix_pre[layer], w_qkv1[i])
            past = cache_k.shape[2]
            a_s = _sb_sample(qsb, ksb, vsb, cache_k[i].reshape(dec_b, past, d),
                             cache_v[i].reshape(dec_b, past, d))
            xs = _out_proj(a_s, xs, ms, g_mix_post[layer], w_out1[i])
            hshape = (SB_HEADS, SB_DH)
            kp_l.append(kp.reshape(kp.shape[:2] + hshape))
            vp_l.append(vp.reshape(vp.shape[:2] + hshape))
            ks_l.append(ks.reshape(ks.shape[:2] + hshape))
            vs_l.append(vs.reshape(vs.shape[:2] + hshape))
        moe_w = _moe_weights(w_grp[layer], b_grp[layer], w_rt[layer], b_rt[layer], w1[layer], w3[layer], w2[layer])
        xp = _moe_layer(xp, mp, g_ffn_pre[layer], g_ffn_post[layer], moe_w)
        xs = _moe_layer(xs, ms, g_ffn_pre[layer], g_ffn_post[layer], moe_w)
    return (xp, xs, jnp.stack(gla_p), jnp.stack(gla_s), jnp.stack(gmv_s),
            jnp.stack(kp_l), jnp.stack(vp_l), jnp.stack(ks_l), jnp.stack(vs_l))
```

```python
import functools
import math

import numpy as np
import jax
import jax.numpy as jnp
from jax import lax
from jax.experimental import pallas as pl
from jax.experimental.pallas import tpu as pltpu

F32 = jnp.float32
BF16 = jnp.bfloat16

EPS = 1e-6
D_MODEL = 1024
GLA_HEADS = 4
GLA_DK = 64
GLA_DV = 128
GLA_RANK = 16
GLA_TAU = 16.0
GLA_QK = GLA_HEADS * GLA_DK
GLA_V = GLA_HEADS * GLA_DV
GM_GROUPS = 4
GM_CH = 128
GM_WIDTH = GM_GROUPS * GM_CH
GM_BLOCK = 128
SB_HEADS = 16
SB_DH = 64
N_GROUPS = 2
EXP_PER_GROUP = 4
N_EXPERTS = 8
D_EXPERT = 256

LANES = 128
RANK_PAD = LANES
IN0_PAD = 2 * GLA_QK + 2 * GLA_V + 2 * GM_WIDTH + RANK_PAD
ROUTER_PAD = LANES
SB_TK = 256
ROW_TILE = 512
L0_TILE = 512
SB_EXIT = 104.0
VMEM_LIMIT = 56 * 1024 * 1024


def _cparams(sem):
    return pltpu.CompilerParams(dimension_semantics=sem, vmem_limit_bytes=VMEM_LIMIT)


def _const_spec(shape):
    return pl.BlockSpec(shape, lambda b, t: (0,) * len(shape), pipeline_mode=pl.Buffered(1))


def _dot(a, b):
    return jnp.dot(a, b, preferred_element_type=F32)


def _dot_nt(a, b):
    return lax.dot_general(a, b, (((1,), (1,)), ((), ())), preferred_element_type=F32)


def _dot_tn(a, b):
    return lax.dot_general(a, b, (((0,), (0,)), ((), ())), preferred_element_type=F32)


def _split2(a):
    hi = a.astype(BF16)
    lo = (a - hi.astype(F32)).astype(BF16)
    return hi, lo


def _split3(a):
    hi = a.astype(BF16)
    r1 = a - hi.astype(F32)
    mid = r1.astype(BF16)
    lo = (r1 - mid.astype(F32)).astype(BF16)
    return hi, mid, lo


def _dot_x2(a, b_hi, b_lo):
    a_hi, a_lo = _split2(a)
    return _dot(a_hi, b_hi) + (_dot(a_lo, b_hi) + _dot(a_hi, b_lo))


def _sigmoid(x):
    return 1.0 / (1.0 + jnp.exp(-x))


def _silu(x):
    return x * _sigmoid(x)


def _softplus(x):
    return jnp.maximum(x, 0.0) + jnp.log(1.0 + jnp.exp(-jnp.abs(x)))


def _gelu_tanh(x):
    c = math.sqrt(2.0 / math.pi)
    return 0.5 * x * (1.0 + jnp.tanh(c * (x + 0.044715 * (x * x * x))))


def _rms(x, gain):
    ms = jnp.mean(x * x, axis=-1, keepdims=True)
    return x * lax.rsqrt(ms + EPS) * gain


def _modulate(x, gain, shift, scale):
    return _rms(x, gain) * (1.0 + scale) + shift


def _mod_kernel(c_ref, w_ref, b_ref, o_ref):
    a = _silu(c_ref[...])
    w_hi, w_lo = _split2(w_ref[0])
    o_ref[0] = _dot_x2(a, w_hi, w_lo) + b_ref[0]


def _modulation(c_all, w_mod, b_mod):
    depth, d, n = w_mod.shape
    rows = c_all.shape[0]
    tn = 1536
    return pl.pallas_call(
        _mod_kernel,
        grid=(depth, n // tn),
        in_specs=[
            pl.BlockSpec((rows, d), lambda l, j: (0, 0)),
            pl.BlockSpec((1, d, tn), lambda l, j: (l, 0, j)),
            pl.BlockSpec((1, 1, tn), lambda l, j: (l, 0, j)),
        ],
        out_specs=pl.BlockSpec((1, rows, tn), lambda l, j: (l, 0, j)),
        out_shape=jax.ShapeDtypeStruct((depth, rows, n), F32),
        compiler_params=_cparams(("arbitrary", "arbitrary")),
        name="modulation",
    )(c_all, w_mod, b_mod.reshape(depth, 1, n))


def _gla_level_matrix(tm):
    nlev = int(math.log2(tm))
    idx = np.arange(tm)
    blocks = []
    for lev in range(nlev):
        h = 1 << lev
        mid = (idx // (2 * h)) * (2 * h) + h
        right = idx >= mid
        m = np.zeros((tm, tm), np.float32)
        for i in range(tm):
            if right[i]:
                m[i, mid[i]:i + 1] = 1.0
            else:
                m[i, i + 1:mid[i]] = 1.0
        blocks.append(m)
    blocks.append(np.tril(np.ones((tm, tm), np.float32)))
    blocks.append(np.triu(np.ones((tm, tm), np.float32), 1))
    return np.concatenate(blocks, axis=0), nlev


def _l0_kernel(x_ref, mod_ref, s0_ref, gpre_ref, gpost_ref, win_ref, wa2h_ref, wa2l_ref, ba_ref,
               glan_ref, lng_ref, lnb_ref, ws_ref, bs_ref, wout_ref, mall_ref,
               x1_ref, snew_ref, *rest, tm, ch, nlev, emit_vg):
    if emit_vg:
        vg_ref, s_scr = rest
    else:
        (s_scr,) = rest
    t = pl.program_id(1)

    @pl.when(t == 0)
    def _():
        s_scr[...] = s0_ref[0]

    x = x_ref[0]
    mod = mod_ref[0]
    h = _modulate(x, gpre_ref[...], mod[0:1], mod[1:2])
    p = _dot(h.astype(BF16), win_ref[...])
    o0 = 0
    q = p[:, o0:o0 + GLA_QK] * (GLA_DK ** -0.5)
    o0 += GLA_QK
    k = p[:, o0:o0 + GLA_QK]
    o0 += GLA_QK
    v = p[:, o0:o0 + GLA_V]
    o0 += GLA_V
    g = p[:, o0:o0 + GLA_V]
    o0 += GLA_V
    u = p[:, o0:o0 + GM_WIDTH]
    o0 += GM_WIDTH
    vgr = p[:, o0:o0 + GM_WIDTH]
    o0 += GM_WIDTH
    r = p[:, o0:o0 + RANK_PAD]

    xa = _dot_x2(r, wa2h_ref[...], wa2l_ref[...]) + ba_ref[...]
    la = -_softplus(-xa) * (1.0 / GLA_TAU)
    u = _gelu_tanh(u)
    vg = _gelu_tanh(vgr)
    mu = jnp.mean(vg, axis=-1, keepdims=True)
    vc = vg - mu
    vg = vc * lax.rsqrt(jnp.mean(vc * vc, axis=-1, keepdims=True) + EPS) * lng_ref[...] + lnb_ref[...]
    if emit_vg:
        vg_ref[0] = vg

    mall = mall_ref[...]
    glan = glan_ref[...]
    bs = bs_ref[...]
    row = lax.broadcasted_iota(jnp.int32, (ch, GLA_QK), 0)
    ii = lax.broadcasted_iota(jnp.int32, (ch, ch), 0)
    jj = lax.broadcasted_iota(jnp.int32, (ch, ch), 1)
    ones = jnp.ones((ch, GLA_DV), BF16)
    wm = [jnp.where(ii >= jj, ws_ref[gg], 0.0).astype(BF16) for gg in range(GM_GROUPS)]

    mixed = []
    for c in range(tm // ch):
        rs = slice(c * ch, (c + 1) * ch)
        qc, kc = q[rs], k[rs]
        la3 = _split3(la[rs])
        e_all = _dot(mall, la3[0]) + (_dot(mall, la3[1]) + _dot(mall, la3[2]))
        qb, kb, vb = qc.astype(BF16), kc.astype(BF16), v[rs].astype(BF16)

        attn = [jnp.where(ii == jj, _dot_nt(qb[:, hh * GLA_DK:(hh + 1) * GLA_DK],
                                            kb[:, hh * GLA_DK:(hh + 1) * GLA_DK]), 0.0)
                for hh in range(GLA_HEADS)]
        for lev in range(nlev):
            w = jnp.exp(e_all[lev * ch:(lev + 1) * ch])
            is_right = ((row >> lev) & 1) == 1
            zz = (jnp.where(is_right, qc, kc) * w).astype(BF16)
            pair = ((ii >> lev) == (jj >> lev) + 1) & (((jj >> lev) & 1) == 0)
            for hh in range(GLA_HEADS):
                zh = zz[:, hh * GLA_DK:(hh + 1) * GLA_DK]
                attn[hh] = attn[hh] + jnp.where(pair, _dot_nt(zh, zh), 0.0)

        b_in = e_all[nlev * ch:(nlev + 1) * ch]
        b_suf = e_all[(nlev + 1) * ch:(nlev + 2) * ch]
        q_in = (qc * jnp.exp(b_in)).astype(BF16)
        k_out = (kc * jnp.exp(b_suf)).astype(BF16)
        dcol = jnp.exp(_dot_tn(la3[0], ones) + (_dot_tn(la3[1], ones) + _dot_tn(la3[2], ones)))

        parts = []
        for hh in range(GLA_HEADS):
            ks = slice(hh * GLA_DK, (hh + 1) * GLA_DK)
            vs = slice(hh * GLA_DV, (hh + 1) * GLA_DV)
            s_h = s_scr[hh]
            o_h = _dot(attn[hh].astype(BF16), vb[:, vs]) + _dot(q_in[:, ks], s_h.astype(BF16))
            s_scr[hh] = dcol[ks] * s_h + _dot_tn(k_out[:, ks], vb[:, vs])
            parts.append(_rms(o_h, glan[:, vs]))

        vgb = vg[rs].astype(BF16)
        for gg in range(GM_GROUPS):
            cs = slice(gg * GM_CH, (gg + 1) * GM_CH)
            parts.append(_dot(wm[gg], vgb[:, cs]) + bs[:, gg:gg + 1])
        mixed.append(jnp.concatenate(parts, axis=-1))

    @pl.when(t == pl.num_programs(1) - 1)
    def _():
        snew_ref[0] = s_scr[...]

    om = mixed[0] if len(mixed) == 1 else jnp.concatenate(mixed, axis=0)
    om = om * jnp.concatenate([_silu(g), u], axis=-1)
    y = _dot(om.astype(BF16), wout_ref[...])
    x1_ref[0] = x + mod[2:3] * _rms(y, gpost_ref[...])


def _layer0_mixer(x, mod, s0, g_pre, g_post, w_in, w_a2, b_a, gla_norm, ln_g, ln_b, w_s, b_s, w_out,
                  emit_vg):
    bsz, seq, d = x.shape
    ch = min(GM_BLOCK, seq)
    tm = min(L0_TILE, seq)
    mall_np, nlev = _gla_level_matrix(ch)
    mall = jnp.asarray(mall_np, BF16)
    o2 = 2 * GLA_QK
    o3 = o2 + GLA_V
    o4 = o3 + GLA_V
    o5 = o4 + GLA_RANK
    o6 = o5 + GM_WIDTH
    w_in_p = jnp.concatenate(
        [w_in[:, :o4], w_in[:, o5:], w_in[:, o4:o5], jnp.zeros((d, RANK_PAD - GLA_RANK), w_in.dtype)],
        axis=1).astype(BF16)
    w_a2_p = jnp.concatenate([w_a2, jnp.zeros((RANK_PAD - GLA_RANK, GLA_QK), w_a2.dtype)], axis=0)
    wa2_hi = w_a2_p.astype(BF16)
    wa2_lo = (w_a2_p - wa2_hi.astype(F32)).astype(BF16)
    ws_t = w_s[:, :ch, :ch]
    bs_t = jnp.transpose(b_s[:, :ch])

    full = _const_spec
    in_specs = [
        pl.BlockSpec((1, tm, d), lambda b, t: (b, t, 0)),
        pl.BlockSpec((1, 8, d), lambda b, t: (b, 0, 0)),
        pl.BlockSpec((1, GLA_HEADS, GLA_DK, GLA_DV), lambda b, t: (b, 0, 0, 0)),
        full((1, d)), full((1, d)),
        full((d, IN0_PAD)),
        full((RANK_PAD, GLA_QK)), full((RANK_PAD, GLA_QK)), full((1, GLA_QK)),
        full((1, GLA_V)), full((1, GM_WIDTH)), full((1, GM_WIDTH)),
        full((GM_GROUPS, ch, ch)), full((ch, GM_GROUPS)),
        full((GLA_V + GM_WIDTH, d)),
        full(((nlev + 2) * ch, ch)),
    ]
    out_specs = [
        pl.BlockSpec((1, tm, d), lambda b, t: (b, t, 0)),
        pl.BlockSpec((1, GLA_HEADS, GLA_DK, GLA_DV), lambda b, t: (b, 0, 0, 0)),
    ]
    out_shape = [
        jax.ShapeDtypeStruct((bsz, seq, d), F32),
        jax.ShapeDtypeStruct((bsz, GLA_HEADS, GLA_DK, GLA_DV), F32),
    ]
    if emit_vg:
        out_specs.append(pl.BlockSpec((1, tm, GM_WIDTH), lambda b, t: (b, t, 0)))
        out_shape.append(jax.ShapeDtypeStruct((bsz, seq, GM_WIDTH), F32))
    return pl.pallas_call(
        functools.partial(_l0_kernel, tm=tm, ch=ch, nlev=nlev, emit_vg=emit_vg),
        grid=(bsz, seq // tm),
        in_specs=in_specs,
        out_specs=out_specs,
        out_shape=out_shape,
        scratch_shapes=[pltpu.VMEM((GLA_HEADS, GLA_DK, GLA_DV), F32)],
        compiler_params=_cparams(("arbitrary", "arbitrary")),
        name="layer0_mixer",
    )(x, mod, s0, g_pre.reshape(1, d), g_post.reshape(1, d), w_in_p, wa2_hi, wa2_lo,
      b_a.reshape(1, GLA_QK), gla_norm.reshape(1, GLA_V), ln_g.reshape(1, GM_WIDTH),
      ln_b.reshape(1, GM_WIDTH), ws_t, bs_t, w_out.astype(BF16), mall)


def _route(logits):
    col = lambda c: logits[:, c:c + 1]
    lg0, lg1 = col(0), col(1)
    mx = jnp.maximum(lg0, lg1)
    e0, e1 = jnp.exp(lg0 - mx), jnp.exp(lg1 - mx)
    den = e0 + e1
    p0, p1 = e0 / den, e1 / den
    grp1 = p1 > p0
    g_top = jnp.where(grp1, p1, p0)
    s = [jnp.where(grp1, col(2 + EXP_PER_GROUP + j), col(2 + j)) for j in range(EXP_PER_GROUP)]
    v1 = jnp.maximum(jnp.maximum(s[0], s[1]), jnp.maximum(s[2], s[3]))
    first, taken = [], None
    for j in range(EXP_PER_GROUP):
        hit = s[j] == v1
        if taken is None:
            first.append(hit)
            taken = hit
        else:
            first.append(hit & jnp.logical_not(taken))
            taken = taken | hit
    neg = jnp.float32(-jnp.inf)
    s2 = [jnp.where(first[j], neg, s[j]) for j in range(EXP_PER_GROUP)]
    v2 = jnp.maximum(jnp.maximum(s2[0], s2[1]), jnp.maximum(s2[2], s2[3]))
    second, taken = [], None
    for j in range(EXP_PER_GROUP):
        hit = (s2[j] == v2) & jnp.logical_not(first[j])
        if taken is None:
            second.append(hit)
            taken = hit
        else:
            second.append(hit & jnp.logical_not(taken))
            taken = taken | hit
    e2 = jnp.exp(v2 - v1)
    w1 = g_top * (1.0 / (1.0 + e2))
    w2 = g_top * (e2 / (1.0 + e2))
    comb = []
    for e in range(N_EXPERTS):
        in_grp = grp1 if e >= EXP_PER_GROUP else jnp.logical_not(grp1)
        j = e % EXP_PER_GROUP
        cw = jnp.where(first[j], w1, 0.0) + jnp.where(second[j], w2, 0.0)
        comb.append(jnp.where(in_grp, cw, 0.0))
    return comb


def _moe_body(x, mod, gpre, gpost, wrh, wrl, br, w1, w3, w2):
    xm = _modulate(x, gpre, mod[3:4], mod[4:5])
    logits = _dot_x2(xm, wrh, wrl) + br
    comb = _route(logits)
    xb = xm.astype(BF16)
    h1 = _dot(xb, w1)
    h3 = _dot(xb, w3)
    hid = _silu(h1) * h3
    hid = jnp.concatenate(
        [hid[:, e * D_EXPERT:(e + 1) * D_EXPERT] * comb[e] for e in range(N_EXPERTS)], axis=-1)
    f = _dot(hid.astype(BF16), w2)
    return x + mod[5:6] * _rms(f, gpost)


def _moe_kernel(x_ref, mod_ref, gpre_ref, gpost_ref, wrh_ref, wrl_ref, br_ref, w1_ref, w3_ref, w2_ref,
                o_ref):
    o_ref[0] = _moe_body(x_ref[0], mod_ref[0], gpre_ref[...], gpost_ref[...], wrh_ref[...],
                         wrl_ref[...], br_ref[...], w1_ref[...], w3_ref[...], w2_ref[...])


def _moe_weights(w_grp, b_grp, w_rt, b_rt, w1, w3, w2):
    d = w_grp.shape[0]
    pad = ROUTER_PAD - N_GROUPS - N_EXPERTS
    wr = jnp.concatenate([w_grp, w_rt, jnp.zeros((d, pad), F32)], axis=1)
    wr_hi = wr.astype(BF16)
    wr_lo = (wr - wr_hi.astype(F32)).astype(BF16)
    br = jnp.concatenate([b_grp, b_rt, jnp.zeros((pad,), F32)]).reshape(1, ROUTER_PAD)
    w1c = jnp.transpose(w1, (1, 0, 2)).reshape(d, N_EXPERTS * D_EXPERT).astype(BF16)
    w3c = jnp.transpose(w3, (1, 0, 2)).reshape(d, N_EXPERTS * D_EXPERT).astype(BF16)
    w2c = w2.reshape(N_EXPERTS * D_EXPERT, d).astype(BF16)
    return wr_hi, wr_lo, br, w1c, w3c, w2c


def _moe_specs(d):
    full = _const_spec
    hid = N_EXPERTS * D_EXPERT
    return [full((1, d)), full((1, d)), full((d, ROUTER_PAD)), full((d, ROUTER_PAD)),
            full((1, ROUTER_PAD)), full((d, hid)), full((d, hid)), full((hid, d))]


def _moe_layer(x, mod, g_pre, g_post, moe_w):
    bsz, seq, d = x.shape
    tm = min(ROW_TILE, seq)
    return pl.pallas_call(
        _moe_kernel,
        grid=(bsz, seq // tm),
        in_specs=[pl.BlockSpec((1, tm, d), lambda b, t: (b, t, 0)),
                  pl.BlockSpec((1, 8, d), lambda b, t: (b, 0, 0))] + _moe_specs(d),
        out_specs=pl.BlockSpec((1, tm, d), lambda b, t: (b, t, 0)),
        out_shape=jax.ShapeDtypeStruct((bsz, seq, d), F32),
        compiler_params=_cparams(("arbitrary", "arbitrary")),
        name="moe",
    )(x, mod, g_pre.reshape(1, d), g_post.reshape(1, d), *moe_w)


def _attn_moe_kernel(a_ref, x_ref, mod_ref, gmix_ref, wo_ref, gpre_ref, gpost_ref, wrh_ref, wrl_ref,
                     br_ref, w1_ref, w3_ref, w2_ref, o_ref):
    mod = mod_ref[0]
    y = _dot(a_ref[0], wo_ref[...])
    x1 = x_ref[0] + mod[2:3] * _rms(y, gmix_ref[...])
    o_ref[0] = _moe_body(x1, mod, gpre_ref[...], gpost_ref[...], wrh_ref[...], wrl_ref[...],
                         br_ref[...], w1_ref[...], w3_ref[...], w2_ref[...])


def _attn_out_moe_layer(attn, x, mod, g_mix_post, w_out, g_pre, g_post, moe_w):
    bsz, seq, d = x.shape
    tm = min(ROW_TILE, seq)
    tile = pl.BlockSpec((1, tm, d), lambda b, t: (b, t, 0))
    return pl.pallas_call(
        _attn_moe_kernel,
        grid=(bsz, seq // tm),
        in_specs=[tile, tile, pl.BlockSpec((1, 8, d), lambda b, t: (b, 0, 0)),
                  _const_spec((1, d)), _const_spec((d, d))] + _moe_specs(d),
        out_specs=tile,
        out_shape=jax.ShapeDtypeStruct((bsz, seq, d), F32),
        compiler_params=_cparams(("arbitrary", "arbitrary")),
        name="attn_out_moe",
    )(attn, x, mod, g_mix_post.reshape(1, d), w_out.astype(BF16), g_pre.reshape(1, d),
      g_post.reshape(1, d), *moe_w)


def _qkv_kernel(x_ref, mod_ref, gpre_ref, w_ref, k_ref, v_ref, qb_ref, kb_ref, vb_ref):
    d = x_ref.shape[-1]
    mod = mod_ref[0]
    h = _modulate(x_ref[0], gpre_ref[...], mod[0:1], mod[1:2])
    p = _dot(h.astype(BF16), w_ref[...])
    k = p[:, d:2 * d]
    v = p[:, 2 * d:3 * d]
    k_ref[0] = k
    v_ref[0] = v
    qb_ref[0] = (p[:, 0:d] * (SB_DH ** -0.5)).astype(BF16)
    kb_ref[0] = k.astype(BF16)
    vb_ref[0] = v.astype(BF16)


def _qkv_proj(x, mod, g_pre, w_qkv):
    bsz, seq, d = x.shape
    tm = min(ROW_TILE, seq)
    tile = pl.BlockSpec((1, tm, d), lambda b, t: (b, t, 0))
    return pl.pallas_call(
        _qkv_kernel,
        grid=(bsz, seq // tm),
        in_specs=[tile, pl.BlockSpec((1, 8, d), lambda b, t: (b, 0, 0)),
                  _const_spec((1, d)), _const_spec((d, 3 * d))],
        out_specs=[tile] * 5,
        out_shape=[jax.ShapeDtypeStruct((bsz, seq, d), F32)] * 2
        + [jax.ShapeDtypeStruct((bsz, seq, d), BF16)] * 3,
        compiler_params=_cparams(("arbitrary", "arbitrary")),
        name="qkv_proj",
    )(x, mod, g_pre.reshape(1, d), w_qkv.astype(BF16))


def _suffix_matrix(n):
    u = np.triu(np.ones((n, n), np.float32)).T
    return jnp.asarray(np.concatenate([u, u], axis=0), BF16)


def _sb_tile(qa, qb, k, v, u2, ca, cb, vis):
    outs = []
    for qh, c in ((qa, ca), (qb, cb)):
        z = _dot_nt(qh, k)
        sp = _softplus(z)
        if vis is not None:
            sp = jnp.where(vis, sp, 0.0)
        hi, lo = _split2(sp)
        w = _dot(jnp.concatenate([hi, lo], axis=-1), u2)
        a = jnp.exp(z - w - c)
        if vis is not None:
            a = jnp.where(vis, a, 0.0)
        outs.append((_dot(a.astype(BF16), v), c + w[:, 0:1]))
    return outs[0][0], outs[1][0], outs[0][1], outs[1][1]


def _sb_rest(tile_at, n_rest, oa, ob, ca, cb):
    def low(ca, cb):
        return jnp.min(jnp.minimum(ca, cb))

    def cond(c):
        return (c[0] < n_rest) & (c[1] <= SB_EXIT)

    def body(c):
        n, _, oa, ob, ca, cb = c
        da, db, ca, cb = tile_at(n, ca, cb)
        return n + 1, low(ca, cb), oa + da, ob + db, ca, cb

    c = lax.while_loop(cond, body, (jnp.int32(0), low(ca, cb), oa, ob, ca, cb))
    return c[2], c[3]


def _sb_prompt_kernel(q_ref, k_ref, v_ref, u2_ref, o_ref, *, tq):
    seq = q_ref.shape[1]
    u2 = u2_ref[...]
    lane = lax.broadcasted_iota(jnp.int32, (tq, LANES), 1)
    first = lane < SB_DH
    ti = lax.broadcasted_iota(jnp.int32, (tq, tq), 0)
    si = lax.broadcasted_iota(jnp.int32, (tq, tq), 1)
    vis = si < ti
    zero = jnp.zeros((tq, 1), F32)

    def rows(ref, blk):
        return ref[0, pl.ds(pl.multiple_of(blk * tq, tq), tq), :]

    def split_heads(qi):
        q = rows(q_ref, qi)
        return jnp.where(first, q, jnp.zeros_like(q)), jnp.where(first, jnp.zeros_like(q), q)

    def store(qi, oa, ob):
        o_ref[0, pl.ds(pl.multiple_of(qi * tq, tq), tq), :] = jnp.where(first, oa, ob).astype(o_ref.dtype)

    qa, qb = split_heads(0)
    oa, ob, _, _ = _sb_tile(qa, qb, rows(k_ref, 0), rows(v_ref, 0), u2, zero, zero, vis)
    store(0, oa, ob)

    def qblock(qi, _):
        qa, qb = split_heads(qi)
        oa, ob, ca, cb = _sb_tile(qa, qb, rows(k_ref, qi), rows(v_ref, qi), u2, zero, zero, vis)
        da, db, ca, cb = _sb_tile(qa, qb, rows(k_ref, qi - 1), rows(v_ref, qi - 1), u2, ca, cb, None)

        def tile_at(n, ca, cb):
            return _sb_tile(qa, qb, rows(k_ref, qi - 2 - n), rows(v_ref, qi - 2 - n), u2, ca, cb, None)

        oa, ob = _sb_rest(tile_at, qi - 1, oa + da, ob + db, ca, cb)
        store(qi, oa, ob)
        return 0

    lax.fori_loop(1, seq // tq, qblock, 0)


def _sb_prompt(qb, kb, vb):
    bsz, seq, d = qb.shape
    tq = min(SB_TK, seq)
    blk = pl.BlockSpec((1, seq, LANES), lambda b, h: (b, 0, h))
    return pl.pallas_call(
        functools.partial(_sb_prompt_kernel, tq=tq),
        grid=(bsz, d // LANES),
        in_specs=[blk, blk, blk, pl.BlockSpec((2 * tq, tq), lambda b, h: (0, 0))],
        out_specs=blk,
        out_shape=jax.ShapeDtypeStruct((bsz, seq, d), BF16),
        compiler_params=_cparams(("arbitrary", "arbitrary")),
        name="sb_prompt",
    )(qb, kb, vb, _suffix_matrix(tq))


def _sb_sample_kernel(q_ref, kn_ref, vn_ref, ckl_ref, cvl_ref, ck_hbm, cv_hbm, un_ref, uc_ref, o_ref,
                      kbuf, vbuf, sem, *, tk, nblk):
    b = pl.program_id(0)
    tq = q_ref.shape[1]
    rows = tk * SB_HEADS
    lane = lax.broadcasted_iota(jnp.int32, (tq, LANES), 1)
    first = lane < SB_DH
    ti = lax.broadcasted_iota(jnp.int32, (tq, tq), 0)
    si = lax.broadcasted_iota(jnp.int32, (tq, tq), 1)
    zero = jnp.zeros((tq, 1), F32)
    un = un_ref[...]
    uc = uc_ref[...]

    def head_pair(load, hp):
        return jnp.concatenate([load(2 * hp), load(2 * hp + 1)], axis=-1).astype(BF16)

    def pair_body(hp, _):
        c0 = pl.multiple_of(hp * LANES, LANES)
        q = q_ref[0, :, pl.ds(c0, LANES)]
        qa = jnp.where(first, q, jnp.zeros_like(q))
        qb = jnp.where(first, jnp.zeros_like(q), q)
        oa, ob, ca, cb = _sb_tile(qa, qb, kn_ref[0, :, pl.ds(c0, LANES)], vn_ref[0, :, pl.ds(c0, LANES)],
                                  un, zero, zero, si < ti)
        kk = head_pair(lambda h: ckl_ref[0, pl.ds(h, tk, stride=SB_HEADS), :], hp)
        vv = head_pair(lambda h: cvl_ref[0, pl.ds(h, tk, stride=SB_HEADS), :], hp)
        da, db, ca, cb = _sb_tile(qa, qb, kk, vv, uc, ca, cb, None)

        def older_tile(n, ca, cb):
            r0 = pl.multiple_of((nblk - 2 - n) * rows, rows)
            ck = pltpu.make_async_copy(ck_hbm.at[b, pl.ds(r0, rows), :], kbuf, sem.at[0])
            cv = pltpu.make_async_copy(cv_hbm.at[b, pl.ds(r0, rows), :], vbuf, sem.at[1])
            ck.start()
            cv.start()
            ck.wait()
            cv.wait()
            kk = head_pair(lambda h: kbuf[pl.ds(h, tk, stride=SB_HEADS), :], hp)
            vv = head_pair(lambda h: vbuf[pl.ds(h, tk, stride=SB_HEADS), :], hp)
            return _sb_tile(qa, qb, kk, vv, uc, ca, cb, None)

        oa, ob = _sb_rest(older_tile, nblk - 1, oa + da, ob + db, ca, cb)
        o_ref[0, :, pl.ds(c0, LANES)] = jnp.where(first, oa, ob).astype(o_ref.dtype)
        return 0

    lax.fori_loop(0, q_ref.shape[2] // LANES, pair_body, 0)


def _sb_sample(qb, kb, vb, cache_k, cache_v):
    bsz, seq, d = qb.shape
    past = cache_k.shape[1] // SB_HEADS
    tk = min(SB_TK, past)
    nblk = past // tk
    rows = tk * SB_HEADS
    blk = pl.BlockSpec((1, seq, d), lambda b: (b, 0, 0))
    last = pl.BlockSpec((1, rows, SB_DH), lambda b: (b, nblk - 1, 0))
    const = lambda shape: pl.BlockSpec(shape, lambda b: (0,) * len(shape))
    return pl.pallas_call(
        functools.partial(_sb_sample_kernel, tk=tk, nblk=nblk),
        grid=(bsz,),
        in_specs=[blk, blk, blk, last, last,
                  pl.BlockSpec(memory_space=pl.ANY), pl.BlockSpec(memory_space=pl.ANY),
                  const((2 * seq, seq)), const((2 * tk, tk))],
        out_specs=blk,
        out_shape=jax.ShapeDtypeStruct((bsz, seq, d), BF16),
        scratch_shapes=[pltpu.VMEM((rows, SB_DH), F32), pltpu.VMEM((rows, SB_DH), F32),
                        pltpu.SemaphoreType.DMA((2,))],
        compiler_params=_cparams(("arbitrary",)),
        name="sb_sample",
    )(qb, kb, vb, cache_k, cache_v, cache_k, cache_v, _suffix_matrix(seq), _suffix_matrix(tk))


def kernel(x_prompt, x_sample, c_prompt, c_sample, state_gla, cache_k, cache_v, w_mod, b_mod, g_mix_pre, g_mix_post, g_ffn_pre, g_ffn_post, w_in0, w_a2, b_a, gla_norm, gm_ln_g, gm_ln_b, w_s, b_s, w_out0, w_qkv1, w_out1, w_grp, b_grp, w_rt, b_rt, w1, w3, w2):
    bp, dec_b = x_prompt.shape[0], x_sample.shape[0]
    d = x_prompt.shape[-1]
    depth = w_mod.shape[0]

    m = _modulation(jnp.concatenate([c_prompt, c_sample], axis=0), w_mod, b_mod)
    m = m.reshape(depth, bp + dec_b, 6, d)
    m = jnp.concatenate([m, jnp.zeros((depth, bp + dec_b, 2, d), F32)], axis=2)

    xp, xs = x_prompt, x_sample
    gla_p, gla_s, gmv_s = [], [], []
    kp_l, vp_l, ks_l, vs_l = [], [], [], []
    for layer in range(depth):
        mp, ms = m[layer, :bp], m[layer, bp:]
        moe_w = _moe_weights(w_grp[layer], b_grp[layer], w_rt[layer], b_rt[layer], w1[layer], w3[layer], w2[layer])
        i = layer // 2
        if layer % 2 == 0:
            ew = (w_in0[i], w_a2[i], b_a[i], gla_norm[i], gm_ln_g[i], gm_ln_b[i], w_s[i], b_s[i], w_out0[i])
            s0 = jnp.zeros((bp, GLA_HEADS, GLA_DK, GLA_DV), state_gla.dtype)
            xp, sp = _layer0_mixer(xp, mp, s0, g_mix_pre[layer], g_mix_post[layer], *ew, emit_vg=False)
            xs, ss, vrows = _layer0_mixer(xs, ms, state_gla[i], g_mix_pre[layer], g_mix_post[layer], *ew,
                                          emit_vg=True)
            gla_p.append(sp)
            gla_s.append(ss)
            gmv_s.append(vrows)
            xp = _moe_layer(xp, mp, g_ffn_pre[layer], g_ffn_post[layer], moe_w)
            xs = _moe_layer(xs, ms, g_ffn_pre[layer], g_ffn_post[layer], moe_w)
        else:
            kp, vp, qpb, kpb, vpb = _qkv_proj(xp, mp, g_mix_pre[layer], w_qkv1[i])
            ap = _sb_prompt(qpb, kpb, vpb)
            ks, vs, qsb, ksb, vsb = _qkv_proj(xs, ms, g_mix_pre[layer], w_qkv1[i])
            past = cache_k.shape[2]
            a_s = _sb_sample(qsb, ksb, vsb, cache_k[i].reshape(dec_b, past * SB_HEADS, SB_DH),
                             cache_v[i].reshape(dec_b, past * SB_HEADS, SB_DH))
            hshape = (SB_HEADS, SB_DH)
            kp_l.append(kp.reshape(kp.shape[:2] + hshape))
            vp_l.append(vp.reshape(vp.shape[:2] + hshape))
            ks_l.append(ks.reshape(ks.shape[:2] + hshape))
            vs_l.append(vs.reshape(vs.shape[:2] + hshape))
            xp = _attn_out_moe_layer(ap, xp, mp, g_mix_post[layer], w_out1[i], g_ffn_pre[layer],
                                     g_ffn_post[layer], moe_w)
            xs = _attn_out_moe_layer(a_s, xs, ms, g_mix_post[layer], w_out1[i], g_ffn_pre[layer],
                                     g_ffn_post[layer], moe_w)
    return (xp, xs, jnp.stack(gla_p), jnp.stack(gla_s), jnp.stack(gmv_s),
            jnp.stack(kp_l), jnp.stack(vp_l), jnp.stack(ks_l), jnp.stack(vs_l))
```

```python
import functools
import math

import numpy as np
import jax
import jax.numpy as jnp
from jax import lax
from jax.experimental import pallas as pl
from jax.experimental.pallas import tpu as pltpu

F32 = jnp.float32
BF16 = jnp.bfloat16

EPS = 1e-6
D_MODEL = 1024
GLA_HEADS = 4
GLA_DK = 64
GLA_DV = 128
GLA_RANK = 16
GLA_TAU = 16.0
GLA_QK = GLA_HEADS * GLA_DK
GLA_V = GLA_HEADS * GLA_DV
GM_GROUPS = 4
GM_CH = 128
GM_WIDTH = GM_GROUPS * GM_CH
GM_BLOCK = 128
SB_HEADS = 16
SB_DH = 64
N_GROUPS = 2
EXP_PER_GROUP = 4
N_EXPERTS = 8
D_EXPERT = 256

LANES = 128
RANK_PAD = LANES
IN0_PAD = 2 * GLA_QK + 2 * GLA_V + 2 * GM_WIDTH + RANK_PAD
ROUTER_PAD = LANES
SB_TK = 256
ROW_TILE = 512
L0_TILE = 512
SB_EXIT = 104.0
VMEM_LIMIT = 56 * 1024 * 1024


def _cparams(sem):
    return pltpu.CompilerParams(dimension_semantics=sem, vmem_limit_bytes=VMEM_LIMIT)


def _const_spec(shape):
    return pl.BlockSpec(shape, lambda b, t: (0,) * len(shape), pipeline_mode=pl.Buffered(1))


def _dot(a, b):
    return jnp.dot(a, b, preferred_element_type=F32)


def _dot_nt(a, b):
    return lax.dot_general(a, b, (((1,), (1,)), ((), ())), preferred_element_type=F32)


def _dot_tn(a, b):
    return lax.dot_general(a, b, (((0,), (0,)), ((), ())), preferred_element_type=F32)


def _split2(a):
    hi = a.astype(BF16)
    lo = (a - hi.astype(F32)).astype(BF16)
    return hi, lo


def _split3(a):
    hi = a.astype(BF16)
    r1 = a - hi.astype(F32)
    mid = r1.astype(BF16)
    lo = (r1 - mid.astype(F32)).astype(BF16)
    return hi, mid, lo


def _dot_x2(a, b_hi, b_lo):
    a_hi, a_lo = _split2(a)
    return _dot(a_hi, b_hi) + (_dot(a_lo, b_hi) + _dot(a_hi, b_lo))


def _sigmoid(x):
    return 1.0 / (1.0 + jnp.exp(-x))


def _silu(x):
    return x * _sigmoid(x)


def _softplus(x):
    return jnp.maximum(x, 0.0) + jnp.log(1.0 + jnp.exp(-jnp.abs(x)))


def _gelu_tanh(x):
    c = math.sqrt(2.0 / math.pi)
    return 0.5 * x * (1.0 + jnp.tanh(c * (x + 0.044715 * (x * x * x))))


def _rms(x, gain):
    ms = jnp.mean(x * x, axis=-1, keepdims=True)
    return x * lax.rsqrt(ms + EPS) * gain


def _modulate(x, gain, shift, scale):
    return _rms(x, gain) * (1.0 + scale) + shift


def _mod_kernel(c_ref, w_ref, b_ref, o_ref):
    a = _silu(c_ref[...])
    w_hi, w_lo = _split2(w_ref[0])
    o_ref[0] = _dot_x2(a, w_hi, w_lo) + b_ref[0]


def _modulation(c_all, w_mod, b_mod):
    depth, d, n = w_mod.shape
    rows = c_all.shape[0]
    tn = 1536
    return pl.pallas_call(
        _mod_kernel,
        grid=(depth, n // tn),
        in_specs=[
            pl.BlockSpec((rows, d), lambda l, j: (0, 0)),
            pl.BlockSpec((1, d, tn), lambda l, j: (l, 0, j)),
            pl.BlockSpec((1, 1, tn), lambda l, j: (l, 0, j)),
        ],
        out_specs=pl.BlockSpec((1, rows, tn), lambda l, j: (l, 0, j)),
        out_shape=jax.ShapeDtypeStruct((depth, rows, n), F32),
        compiler_params=_cparams(("arbitrary", "arbitrary")),
        name="modulation",
    )(c_all, w_mod, b_mod.reshape(depth, 1, n))


def _l0_kernel(x_ref, mod_ref, s0_ref, gpre_ref, gpost_ref, win_ref, wa2h_ref, wa2l_ref, ba_ref,
               glan_ref, lng_ref, lnb_ref, ws_ref, bs_ref, wout_ref,
               x1_ref, snew_ref, *rest, tm, ch, nlev, emit_vg):
    if emit_vg:
        vg_ref, s_scr = rest
    else:
        (s_scr,) = rest
    t = pl.program_id(1)

    @pl.when(t == 0)
    def _():
        s_scr[...] = s0_ref[0]

    x = x_ref[0]
    mod = mod_ref[0]
    h = _modulate(x, gpre_ref[...], mod[0:1], mod[1:2])
    p = _dot(h.astype(BF16), win_ref[...])
    o0 = 0
    q = p[:, o0:o0 + GLA_QK] * (GLA_DK ** -0.5)
    o0 += GLA_QK
    k = p[:, o0:o0 + GLA_QK]
    o0 += GLA_QK
    v = p[:, o0:o0 + GLA_V]
    o0 += GLA_V
    g = p[:, o0:o0 + GLA_V]
    o0 += GLA_V
    u = p[:, o0:o0 + GM_WIDTH]
    o0 += GM_WIDTH
    vgr = p[:, o0:o0 + GM_WIDTH]
    o0 += GM_WIDTH
    r = p[:, o0:o0 + RANK_PAD]

    xa = _dot_x2(r, wa2h_ref[...], wa2l_ref[...]) + ba_ref[...]
    la = -_softplus(-xa) * (1.0 / GLA_TAU)
    u = _gelu_tanh(u)
    vg = _gelu_tanh(vgr)
    mu = jnp.mean(vg, axis=-1, keepdims=True)
    vc = vg - mu
    vg = vc * lax.rsqrt(jnp.mean(vc * vc, axis=-1, keepdims=True) + EPS) * lng_ref[...] + lnb_ref[...]
    if emit_vg:
        vg_ref[0] = vg

    glan = glan_ref[...]
    bs = bs_ref[...]
    row = lax.broadcasted_iota(jnp.int32, (ch, GLA_QK), 0)
    ii = lax.broadcasted_iota(jnp.int32, (ch, ch), 0)
    jj = lax.broadcasted_iota(jnp.int32, (ch, ch), 1)
    wm = [jnp.where(ii >= jj, ws_ref[gg], 0.0).astype(BF16) for gg in range(GM_GROUPS)]

    mixed = []
    for c in range(tm // ch):
        rs = slice(c * ch, (c + 1) * ch)
        qc, kc = q[rs], k[rs]
        b_in = la[rs]
        for lev in range(nlev):
            b_in = b_in + jnp.where(row >= (1 << lev), pltpu.roll(b_in, 1 << lev, 0), 0.0)
        qb, kb, vb = qc.astype(BF16), kc.astype(BF16), v[rs].astype(BF16)
        blk_end = b_in

        attn = [jnp.where(ii == jj, _dot_nt(qb[:, hh * GLA_DK:(hh + 1) * GLA_DK],
                                            kb[:, hh * GLA_DK:(hh + 1) * GLA_DK]), 0.0)
                for hh in range(GLA_HEADS)]
        for lev in range(nlev):
            h = 1 << lev
            is_right = ((row >> lev) & 1) == 1
            w = jnp.exp(jnp.where(is_right, b_in - pltpu.roll(blk_end, h, 0), blk_end - b_in))
            blk_end = jnp.where(is_right, blk_end, pltpu.roll(blk_end, ch - h, 0))
            zz = (jnp.where(is_right, qc, kc) * w).astype(BF16)
            pair = ((ii >> lev) == (jj >> lev) + 1) & (((jj >> lev) & 1) == 0)
            for hh in range(GLA_HEADS):
                zh = zz[:, hh * GLA_DK:(hh + 1) * GLA_DK]
                attn[hh] = attn[hh] + jnp.where(pair, _dot_nt(zh, zh), 0.0)

        q_in = (qc * jnp.exp(b_in)).astype(BF16)
        k_out = (kc * jnp.exp(blk_end - b_in)).astype(BF16)
        total = blk_end if ch == GLA_DV else jnp.concatenate([blk_end] * (GLA_DV // ch), axis=0)
        dcol = jnp.exp(total.T)

        parts = []
        for hh in range(GLA_HEADS):
            ks = slice(hh * GLA_DK, (hh + 1) * GLA_DK)
            vs = slice(hh * GLA_DV, (hh + 1) * GLA_DV)
            s_h = s_scr[hh]
            o_h = _dot(attn[hh].astype(BF16), vb[:, vs]) + _dot(q_in[:, ks], s_h.astype(BF16))
            s_scr[hh] = dcol[ks] * s_h + _dot_tn(k_out[:, ks], vb[:, vs])
            parts.append(_rms(o_h, glan[:, vs]))

        vgb = vg[rs].astype(BF16)
        for gg in range(GM_GROUPS):
            cs = slice(gg * GM_CH, (gg + 1) * GM_CH)
            parts.append(_dot(wm[gg], vgb[:, cs]) + bs[:, gg:gg + 1])
        mixed.append(jnp.concatenate(parts, axis=-1))

    @pl.when(t == pl.num_programs(1) - 1)
    def _():
        snew_ref[0] = s_scr[...]

    om = mixed[0] if len(mixed) == 1 else jnp.concatenate(mixed, axis=0)
    om = om * jnp.concatenate([_silu(g), u], axis=-1)
    y = _dot(om.astype(BF16), wout_ref[...])
    x1_ref[0] = x + mod[2:3] * _rms(y, gpost_ref[...])


def _layer0_mixer(x, mod, s0, g_pre, g_post, w_in, w_a2, b_a, gla_norm, ln_g, ln_b, w_s, b_s, w_out,
                  emit_vg):
    bsz, seq, d = x.shape
    ch = min(GM_BLOCK, seq)
    tm = min(L0_TILE, seq)
    nlev = int(math.log2(ch))
    o2 = 2 * GLA_QK
    o3 = o2 + GLA_V
    o4 = o3 + GLA_V
    o5 = o4 + GLA_RANK
    o6 = o5 + GM_WIDTH
    w_in_p = jnp.concatenate(
        [w_in[:, :o4], w_in[:, o5:], w_in[:, o4:o5], jnp.zeros((d, RANK_PAD - GLA_RANK), w_in.dtype)],
        axis=1).astype(BF16)
    w_a2_p = jnp.concatenate([w_a2, jnp.zeros((RANK_PAD - GLA_RANK, GLA_QK), w_a2.dtype)], axis=0)
    wa2_hi = w_a2_p.astype(BF16)
    wa2_lo = (w_a2_p - wa2_hi.astype(F32)).astype(BF16)
    ws_t = w_s[:, :ch, :ch]
    bs_t = jnp.transpose(b_s[:, :ch])

    full = _const_spec
    in_specs = [
        pl.BlockSpec((1, tm, d), lambda b, t: (b, t, 0)),
        pl.BlockSpec((1, 8, d), lambda b, t: (b, 0, 0)),
        pl.BlockSpec((1, GLA_HEADS, GLA_DK, GLA_DV), lambda b, t: (b, 0, 0, 0)),
        full((1, d)), full((1, d)),
        full((d, IN0_PAD)),
        full((RANK_PAD, GLA_QK)), full((RANK_PAD, GLA_QK)), full((1, GLA_QK)),
        full((1, GLA_V)), full((1, GM_WIDTH)), full((1, GM_WIDTH)),
        full((GM_GROUPS, ch, ch)), full((ch, GM_GROUPS)),
        full((GLA_V + GM_WIDTH, d)),
    ]
    out_specs = [
        pl.BlockSpec((1, tm, d), lambda b, t: (b, t, 0)),
        pl.BlockSpec((1, GLA_HEADS, GLA_DK, GLA_DV), lambda b, t: (b, 0, 0, 0)),
    ]
    out_shape = [
        jax.ShapeDtypeStruct((bsz, seq, d), F32),
        jax.ShapeDtypeStruct((bsz, GLA_HEADS, GLA_DK, GLA_DV), F32),
    ]
    if emit_vg:
        out_specs.append(pl.BlockSpec((1, tm, GM_WIDTH), lambda b, t: (b, t, 0)))
        out_shape.append(jax.ShapeDtypeStruct((bsz, seq, GM_WIDTH), F32))
    return pl.pallas_call(
        functools.partial(_l0_kernel, tm=tm, ch=ch, nlev=nlev, emit_vg=emit_vg),
        grid=(bsz, seq // tm),
        in_specs=in_specs,
        out_specs=out_specs,
        out_shape=out_shape,
        scratch_shapes=[pltpu.VMEM((GLA_HEADS, GLA_DK, GLA_DV), F32)],
        compiler_params=_cparams(("arbitrary", "arbitrary")),
        name="layer0_mixer",
    )(x, mod, s0, g_pre.reshape(1, d), g_post.reshape(1, d), w_in_p, wa2_hi, wa2_lo,
      b_a.reshape(1, GLA_QK), gla_norm.reshape(1, GLA_V), ln_g.reshape(1, GM_WIDTH),
      ln_b.reshape(1, GM_WIDTH), ws_t, bs_t, w_out.astype(BF16))


def _route(logits):
    col = lambda c: logits[:, c:c + 1]
    lg0, lg1 = col(0), col(1)
    mx = jnp.maximum(lg0, lg1)
    e0, e1 = jnp.exp(lg0 - mx), jnp.exp(lg1 - mx)
    den = e0 + e1
    p0, p1 = e0 / den, e1 / den
    grp1 = p1 > p0
    g_top = jnp.where(grp1, p1, p0)
    s = [jnp.where(grp1, col(2 + EXP_PER_GROUP + j), col(2 + j)) for j in range(EXP_PER_GROUP)]
    v1 = jnp.maximum(jnp.maximum(s[0], s[1]), jnp.maximum(s[2], s[3]))
    first, taken = [], None
    for j in range(EXP_PER_GROUP):
        hit = s[j] == v1
        if taken is None:
            first.append(hit)
            taken = hit
        else:
            first.append(hit & jnp.logical_not(taken))
            taken = taken | hit
    neg = jnp.float32(-jnp.inf)
    s2 = [jnp.where(first[j], neg, s[j]) for j in range(EXP_PER_GROUP)]
    v2 = jnp.maximum(jnp.maximum(s2[0], s2[1]), jnp.maximum(s2[2], s2[3]))
    second, taken = [], None
    for j in range(EXP_PER_GROUP):
        hit = (s2[j] == v2) & jnp.logical_not(first[j])
        if taken is None:
            second.append(hit)
            taken = hit
        else:
            second.append(hit & jnp.logical_not(taken))
            taken = taken | hit
    e2 = jnp.exp(v2 - v1)
    w1 = g_top * (1.0 / (1.0 + e2))
    w2 = g_top * (e2 / (1.0 + e2))
    comb = []
    for e in range(N_EXPERTS):
        in_grp = grp1 if e >= EXP_PER_GROUP else jnp.logical_not(grp1)
        j = e % EXP_PER_GROUP
        cw = jnp.where(first[j], w1, 0.0) + jnp.where(second[j], w2, 0.0)
        comb.append(jnp.where(in_grp, cw, 0.0))
    return comb


def _moe_body(x, mod, gpre, gpost, wrh, wrl, br, w1, w3, w2):
    xm = _modulate(x, gpre, mod[3:4], mod[4:5])
    logits = _dot_x2(xm, wrh, wrl) + br
    comb = _route(logits)
    xb = xm.astype(BF16)
    h1 = _dot(xb, w1)
    h3 = _dot(xb, w3)
    hid = _silu(h1) * h3
    hid = jnp.concatenate(
        [hid[:, e * D_EXPERT:(e + 1) * D_EXPERT] * comb[e] for e in range(N_EXPERTS)], axis=-1)
    f = _dot(hid.astype(BF16), w2)
    return x + mod[5:6] * _rms(f, gpost)


def _moe_kernel(x_ref, mod_ref, gpre_ref, gpost_ref, wrh_ref, wrl_ref, br_ref, w1_ref, w3_ref, w2_ref,
                o_ref):
    o_ref[0] = _moe_body(x_ref[0], mod_ref[0], gpre_ref[...], gpost_ref[...], wrh_ref[...],
                         wrl_ref[...], br_ref[...], w1_ref[...], w3_ref[...], w2_ref[...])


def _moe_weights(w_grp, b_grp, w_rt, b_rt, w1, w3, w2):
    d = w_grp.shape[0]
    pad = ROUTER_PAD - N_GROUPS - N_EXPERTS
    wr = jnp.concatenate([w_grp, w_rt, jnp.zeros((d, pad), F32)], axis=1)
    wr_hi = wr.astype(BF16)
    wr_lo = (wr - wr_hi.astype(F32)).astype(BF16)
    br = jnp.concatenate([b_grp, b_rt, jnp.zeros((pad,), F32)]).reshape(1, ROUTER_PAD)
    w1c = jnp.transpose(w1, (1, 0, 2)).reshape(d, N_EXPERTS * D_EXPERT).astype(BF16)
    w3c = jnp.transpose(w3, (1, 0, 2)).reshape(d, N_EXPERTS * D_EXPERT).astype(BF16)
    w2c = w2.reshape(N_EXPERTS * D_EXPERT, d).astype(BF16)
    return wr_hi, wr_lo, br, w1c, w3c, w2c


def _moe_specs(d):
    full = _const_spec
    hid = N_EXPERTS * D_EXPERT
    return [full((1, d)), full((1, d)), full((d, ROUTER_PAD)), full((d, ROUTER_PAD)),
            full((1, ROUTER_PAD)), full((d, hid)), full((d, hid)), full((hid, d))]


def _moe_layer(x, mod, g_pre, g_post, moe_w):
    bsz, seq, d = x.shape
    tm = min(ROW_TILE, seq)
    return pl.pallas_call(
        _moe_kernel,
        grid=(bsz, seq // tm),
        in_specs=[pl.BlockSpec((1, tm, d), lambda b, t: (b, t, 0)),
                  pl.BlockSpec((1, 8, d), lambda b, t: (b, 0, 0))] + _moe_specs(d),
        out_specs=pl.BlockSpec((1, tm, d), lambda b, t: (b, t, 0)),
        out_shape=jax.ShapeDtypeStruct((bsz, seq, d), F32),
        compiler_params=_cparams(("arbitrary", "arbitrary")),
        name="moe",
    )(x, mod, g_pre.reshape(1, d), g_post.reshape(1, d), *moe_w)


def _attn_moe_kernel(a_ref, x_ref, mod_ref, gmix_ref, wo_ref, gpre_ref, gpost_ref, wrh_ref, wrl_ref,
                     br_ref, w1_ref, w3_ref, w2_ref, o_ref):
    mod = mod_ref[0]
    y = _dot(a_ref[0], wo_ref[...])
    x1 = x_ref[0] + mod[2:3] * _rms(y, gmix_ref[...])
    o_ref[0] = _moe_body(x1, mod, gpre_ref[...], gpost_ref[...], wrh_ref[...], wrl_ref[...],
                         br_ref[...], w1_ref[...], w3_ref[...], w2_ref[...])


def _attn_out_moe_layer(attn, x, mod, g_mix_post, w_out, g_pre, g_post, moe_w):
    bsz, seq, d = x.shape
    tm = min(ROW_TILE, seq)
    tile = pl.BlockSpec((1, tm, d), lambda b, t: (b, t, 0))
    return pl.pallas_call(
        _attn_moe_kernel,
        grid=(bsz, seq // tm),
        in_specs=[tile, tile, pl.BlockSpec((1, 8, d), lambda b, t: (b, 0, 0)),
                  _const_spec((1, d)), _const_spec((d, d))] + _moe_specs(d),
        out_specs=tile,
        out_shape=jax.ShapeDtypeStruct((bsz, seq, d), F32),
        compiler_params=_cparams(("arbitrary", "arbitrary")),
        name="attn_out_moe",
    )(attn, x, mod, g_mix_post.reshape(1, d), w_out.astype(BF16), g_pre.reshape(1, d),
      g_post.reshape(1, d), *moe_w)


def _qkv_kernel(x_ref, mod_ref, gpre_ref, w_ref, k_ref, v_ref, qb_ref, kb_ref, vb_ref):
    d = x_ref.shape[-1]
    mod = mod_ref[0]
    h = _modulate(x_ref[0], gpre_ref[...], mod[0:1], mod[1:2])
    p = _dot(h.astype(BF16), w_ref[...])
    k = p[:, d:2 * d]
    v = p[:, 2 * d:3 * d]
    k_ref[0] = k
    v_ref[0] = v
    qb_ref[0] = (p[:, 0:d] * (SB_DH ** -0.5)).astype(BF16)
    kb_ref[0] = k.astype(BF16)
    vb_ref[0] = v.astype(BF16)


def _qkv_proj(x, mod, g_pre, w_qkv):
    bsz, seq, d = x.shape
    tm = min(ROW_TILE, seq)
    tile = pl.BlockSpec((1, tm, d), lambda b, t: (b, t, 0))
    return pl.pallas_call(
        _qkv_kernel,
        grid=(bsz, seq // tm),
        in_specs=[tile, pl.BlockSpec((1, 8, d), lambda b, t: (b, 0, 0)),
                  _const_spec((1, d)), _const_spec((d, 3 * d))],
        out_specs=[tile] * 5,
        out_shape=[jax.ShapeDtypeStruct((bsz, seq, d), F32)] * 2
        + [jax.ShapeDtypeStruct((bsz, seq, d), BF16)] * 3,
        compiler_params=_cparams(("arbitrary", "arbitrary")),
        name="qkv_proj",
    )(x, mod, g_pre.reshape(1, d), w_qkv.astype(BF16))


def _suffix_matrix(n):
    u = np.triu(np.ones((n, n), np.float32)).T
    return jnp.asarray(np.concatenate([u, u], axis=0), BF16)


def _sb_tile(qa, qb, k, v, u2, ca, cb, vis, feature_major=False):
    outs = []
    for qh, c in ((qa, ca), (qb, cb)):
        z = _dot(qh, k) if feature_major else _dot_nt(qh, k)
        sp = _softplus(z)
        if vis is not None:
            sp = jnp.where(vis, sp, 0.0)
        hi, lo = _split2(sp)
        w = _dot(jnp.concatenate([hi, lo], axis=-1), u2)
        a = jnp.exp(z - w - c)
        if vis is not None:
            a = jnp.where(vis, a, 0.0)
        ab = a.astype(BF16)
        outs.append((_dot_nt(ab, v) if feature_major else _dot(ab, v), c + w[:, 0:1]))
    return outs[0][0], outs[1][0], outs[0][1], outs[1][1]


def _sb_rest(tile_at, n_rest, oa, ob, ca, cb):
    def low(ca, cb):
        return jnp.min(jnp.minimum(ca, cb))

    def cond(c):
        return (c[0] < n_rest) & (c[1] <= SB_EXIT)

    def body(c):
        n, _, oa, ob, ca, cb = c
        da, db, ca, cb = tile_at(n, ca, cb)
        return n + 1, low(ca, cb), oa + da, ob + db, ca, cb

    c = lax.while_loop(cond, body, (jnp.int32(0), low(ca, cb), oa, ob, ca, cb))
    return c[2], c[3]


def _sb_prompt_kernel(q_ref, k_ref, v_ref, u2_ref, o_ref, *, tq):
    seq = q_ref.shape[1]
    u2 = u2_ref[...]
    lane = lax.broadcasted_iota(jnp.int32, (tq, LANES), 1)
    first = lane < SB_DH
    ti = lax.broadcasted_iota(jnp.int32, (tq, tq), 0)
    si = lax.broadcasted_iota(jnp.int32, (tq, tq), 1)
    vis = si < ti
    zero = jnp.zeros((tq, 1), F32)

    def rows(ref, blk):
        return ref[0, pl.ds(pl.multiple_of(blk * tq, tq), tq), :]

    def split_heads(qi):
        q = rows(q_ref, qi)
        return jnp.where(first, q, jnp.zeros_like(q)), jnp.where(first, jnp.zeros_like(q), q)

    def store(qi, oa, ob):
        o_ref[0, pl.ds(pl.multiple_of(qi * tq, tq), tq), :] = jnp.where(first, oa, ob).astype(o_ref.dtype)

    qa, qb = split_heads(0)
    oa, ob, _, _ = _sb_tile(qa, qb, rows(k_ref, 0), rows(v_ref, 0), u2, zero, zero, vis)
    store(0, oa, ob)

    def qblock(qi, _):
        qa, qb = split_heads(qi)
        oa, ob, ca, cb = _sb_tile(qa, qb, rows(k_ref, qi), rows(v_ref, qi), u2, zero, zero, vis)
        da, db, ca, cb = _sb_tile(qa, qb, rows(k_ref, qi - 1), rows(v_ref, qi - 1), u2, ca, cb, None)

        def tile_at(n, ca, cb):
            return _sb_tile(qa, qb, rows(k_ref, qi - 2 - n), rows(v_ref, qi - 2 - n), u2, ca, cb, None)

        oa, ob = _sb_rest(tile_at, qi - 1, oa + da, ob + db, ca, cb)
        store(qi, oa, ob)
        return 0

    lax.fori_loop(1, seq // tq, qblock, 0)


def _sb_prompt(qb, kb, vb):
    bsz, seq, d = qb.shape
    tq = min(SB_TK, seq)
    blk = pl.BlockSpec((1, seq, LANES), lambda b, h: (b, 0, h))
    return pl.pallas_call(
        functools.partial(_sb_prompt_kernel, tq=tq),
        grid=(bsz, d // LANES),
        in_specs=[blk, blk, blk, pl.BlockSpec((2 * tq, tq), lambda b, h: (0, 0))],
        out_specs=blk,
        out_shape=jax.ShapeDtypeStruct((bsz, seq, d), BF16),
        compiler_params=_cparams(("arbitrary", "arbitrary")),
        name="sb_prompt",
    )(qb, kb, vb, _suffix_matrix(tq))


def _sb_sample_kernel(q_ref, kn_ref, vn_ref, ckl_ref, cvl_ref, ck_hbm, cv_hbm, un_ref, uc_ref, o_ref,
                      kbuf, vbuf, sem, *, tk, nblk):
    b = pl.program_id(0)
    tq = q_ref.shape[1]
    lane = lax.broadcasted_iota(jnp.int32, (tq, LANES), 1)
    first = lane < SB_DH
    ti = lax.broadcasted_iota(jnp.int32, (tq, tq), 0)
    si = lax.broadcasted_iota(jnp.int32, (tq, tq), 1)
    zero = jnp.zeros((tq, 1), F32)
    un = un_ref[...]
    uc = uc_ref[...]

    def head_pair(blk):
        return blk.reshape(LANES, tk).astype(BF16)

    def pair_body(hp, _):
        c0 = pl.multiple_of(hp * LANES, LANES)
        q = q_ref[0, :, pl.ds(c0, LANES)]
        qa = jnp.where(first, q, jnp.zeros_like(q))
        qb = jnp.where(first, jnp.zeros_like(q), q)
        oa, ob, ca, cb = _sb_tile(qa, qb, kn_ref[0, :, pl.ds(c0, LANES)], vn_ref[0, :, pl.ds(c0, LANES)],
                                  un, zero, zero, si < ti)
        da, db, ca, cb = _sb_tile(qa, qb, head_pair(ckl_ref[0, pl.ds(2 * hp, 2)]),
                                  head_pair(cvl_ref[0, pl.ds(2 * hp, 2)]), uc, ca, cb, None,
                                  feature_major=True)

        def older_tile(n, ca, cb):
            k0 = pl.multiple_of((nblk - 2 - n) * tk, tk)
            ck = pltpu.make_async_copy(ck_hbm.at[b, pl.ds(2 * hp, 2), :, pl.ds(k0, tk)], kbuf, sem.at[0])
            cv = pltpu.make_async_copy(cv_hbm.at[b, pl.ds(2 * hp, 2), :, pl.ds(k0, tk)], vbuf, sem.at[1])
            ck.start()
            cv.start()
            ck.wait()
            cv.wait()
            return _sb_tile(qa, qb, head_pair(kbuf[...]), head_pair(vbuf[...]), uc, ca, cb, None,
                            feature_major=True)

        oa, ob = _sb_rest(older_tile, nblk - 1, oa + da, ob + db, ca, cb)
        o_ref[0, :, pl.ds(c0, LANES)] = jnp.where(first, oa, ob).astype(o_ref.dtype)
        return 0

    lax.fori_loop(0, q_ref.shape[2] // LANES, pair_body, 0)


def _sb_sample(qb, kb, vb, cache_k, cache_v):
    bsz, seq, d = qb.shape
    past = cache_k.shape[3]
    tk = min(SB_TK, past)
    nblk = past // tk
    blk = pl.BlockSpec((1, seq, d), lambda b: (b, 0, 0))
    last = pl.BlockSpec((1, SB_HEADS, SB_DH, tk), lambda b: (b, 0, 0, nblk - 1))
    const = lambda shape: pl.BlockSpec(shape, lambda b: (0,) * len(shape))
    return pl.pallas_call(
        functools.partial(_sb_sample_kernel, tk=tk, nblk=nblk),
        grid=(bsz,),
        in_specs=[blk, blk, blk, last, last,
                  pl.BlockSpec(memory_space=pl.ANY), pl.BlockSpec(memory_space=pl.ANY),
                  const((2 * seq, seq)), const((2 * tk, tk))],
        out_specs=blk,
        out_shape=jax.ShapeDtypeStruct((bsz, seq, d), BF16),
        scratch_shapes=[pltpu.VMEM((2, SB_DH, tk), F32), pltpu.VMEM((2, SB_DH, tk), F32),
                        pltpu.SemaphoreType.DMA((2,))],
        compiler_params=_cparams(("arbitrary",)),
        name="sb_sample",
    )(qb, kb, vb, cache_k, cache_v, cache_k, cache_v, _suffix_matrix(seq), _suffix_matrix(tk))


def kernel(x_prompt, x_sample, c_prompt, c_sample, state_gla, cache_k, cache_v, w_mod, b_mod, g_mix_pre, g_mix_post, g_ffn_pre, g_ffn_post, w_in0, w_a2, b_a, gla_norm, gm_ln_g, gm_ln_b, w_s, b_s, w_out0, w_qkv1, w_out1, w_grp, b_grp, w_rt, b_rt, w1, w3, w2):
    bp, dec_b = x_prompt.shape[0], x_sample.shape[0]
    d = x_prompt.shape[-1]
    depth = w_mod.shape[0]

    m = _modulation(jnp.concatenate([c_prompt, c_sample], axis=0), w_mod, b_mod)
    m = m.reshape(depth, bp + dec_b, 6, d)
    m = jnp.concatenate([m, jnp.zeros((depth, bp + dec_b, 2, d), F32)], axis=2)

    xp, xs = x_prompt, x_sample
    gla_p, gla_s, gmv_s = [], [], []
    kp_l, vp_l, ks_l, vs_l = [], [], [], []
    for layer in range(depth):
        mp, ms = m[layer, :bp], m[layer, bp:]
        moe_w = _moe_weights(w_grp[layer], b_grp[layer], w_rt[layer], b_rt[layer], w1[layer], w3[layer], w2[layer])
        i = layer // 2
        if layer % 2 == 0:
            ew = (w_in0[i], w_a2[i], b_a[i], gla_norm[i], gm_ln_g[i], gm_ln_b[i], w_s[i], b_s[i], w_out0[i])
            s0 = jnp.zeros((bp, GLA_HEADS, GLA_DK, GLA_DV), state_gla.dtype)
            xp, sp = _layer0_mixer(xp, mp, s0, g_mix_pre[layer], g_mix_post[layer], *ew, emit_vg=False)
            xs, ss, vrows = _layer0_mixer(xs, ms, state_gla[i], g_mix_pre[layer], g_mix_post[layer], *ew,
                                          emit_vg=True)
            gla_p.append(sp)
            gla_s.append(ss)
            gmv_s.append(vrows)
            xp = _moe_layer(xp, mp, g_ffn_pre[layer], g_ffn_post[layer], moe_w)
            xs = _moe_layer(xs, ms, g_ffn_pre[layer], g_ffn_post[layer], moe_w)
        else:
            kp, vp, qpb, kpb, vpb = _qkv_proj(xp, mp, g_mix_pre[layer], w_qkv1[i])
            ap = _sb_prompt(qpb, kpb, vpb)
            ks, vs, qsb, ksb, vsb = _qkv_proj(xs, ms, g_mix_pre[layer], w_qkv1[i])
            past = cache_k.shape[2]
            a_s = _sb_sample(qsb, ksb, vsb, jnp.transpose(cache_k[i], (0, 2, 3, 1)),
                             jnp.transpose(cache_v[i], (0, 2, 3, 1)))
            hshape = (SB_HEADS, SB_DH)
            kp_l.append(kp.reshape(kp.shape[:2] + hshape))
            vp_l.append(vp.reshape(vp.shape[:2] + hshape))
            ks_l.append(ks.reshape(ks.shape[:2] + hshape))
            vs_l.append(vs.reshape(vs.shape[:2] + hshape))
            xp = _attn_out_moe_layer(ap, xp, mp, g_mix_post[layer], w_out1[i], g_ffn_pre[layer],
                                     g_ffn_post[layer], moe_w)
            xs = _attn_out_moe_layer(a_s, xs, ms, g_mix_post[layer], w_out1[i], g_ffn_pre[layer],
                                     g_ffn_post[layer], moe_w)
    return (xp, xs, jnp.stack(gla_p), jnp.stack(gla_s), jnp.stack(gmv_s),
            jnp.stack(kp_l), jnp.stack(vp_l), jnp.stack(ks_l), jnp.stack(vs_l))
```

```python
import functools
import math

import numpy as np
import jax
import jax.numpy as jnp
from jax import lax
from jax.experimental import pallas as pl
from jax.experimental.pallas import tpu as pltpu

F32 = jnp.float32
BF16 = jnp.bfloat16

EPS = 1e-6
D_MODEL = 1024
GLA_HEADS = 4
GLA_DK = 64
GLA_DV = 128
GLA_RANK = 16
GLA_TAU = 16.0
GLA_QK = GLA_HEADS * GLA_DK
GLA_V = GLA_HEADS * GLA_DV
GM_GROUPS = 4
GM_CH = 128
GM_WIDTH = GM_GROUPS * GM_CH
GM_BLOCK = 128
SB_HEADS = 16
SB_DH = 64
N_GROUPS = 2
EXP_PER_GROUP = 4
N_EXPERTS = 8
D_EXPERT = 256

LANES = 128
RANK_PAD = LANES
IN0_PAD = 2 * GLA_QK + 2 * GLA_V + 2 * GM_WIDTH + RANK_PAD
ROUTER_PAD = LANES
SB_TK = 256
ROW_TILE = 512
L0_TILE = 512
SB_EXIT = 104.0
SB_ZCAP = 30.0
VMEM_LIMIT = 56 * 1024 * 1024


def _cparams(sem):
    return pltpu.CompilerParams(dimension_semantics=sem, vmem_limit_bytes=VMEM_LIMIT)


def _const_spec(shape):
    return pl.BlockSpec(shape, lambda b, t: (0,) * len(shape), pipeline_mode=pl.Buffered(1))


def _dot(a, b):
    return jnp.dot(a, b, preferred_element_type=F32)


def _dot_nt(a, b):
    return lax.dot_general(a, b, (((1,), (1,)), ((), ())), preferred_element_type=F32)


def _dot_tn(a, b):
    return lax.dot_general(a, b, (((0,), (0,)), ((), ())), preferred_element_type=F32)


def _split2(a):
    hi = a.astype(BF16)
    lo = (a - hi.astype(F32)).astype(BF16)
    return hi, lo


def _split3(a):
    hi = a.astype(BF16)
    r1 = a - hi.astype(F32)
    mid = r1.astype(BF16)
    lo = (r1 - mid.astype(F32)).astype(BF16)
    return hi, mid, lo


def _dot_x2(a, b_hi, b_lo):
    a_hi, a_lo = _split2(a)
    return _dot(a_hi, b_hi) + (_dot(a_lo, b_hi) + _dot(a_hi, b_lo))


def _sigmoid(x):
    return 1.0 / (1.0 + jnp.exp(-x))


def _silu(x):
    return x * _sigmoid(x)


def _softplus(x):
    return jnp.maximum(x, 0.0) + jnp.log(1.0 + jnp.exp(-jnp.abs(x)))


def _gelu_tanh(x):
    c = math.sqrt(2.0 / math.pi)
    return 0.5 * x * (1.0 + jnp.tanh(c * (x + 0.044715 * (x * x * x))))


def _rms(x, gain):
    ms = jnp.mean(x * x, axis=-1, keepdims=True)
    return x * lax.rsqrt(ms + EPS) * gain


def _modulate(x, gain, shift, scale):
    return _rms(x, gain) * (1.0 + scale) + shift


def _mod_kernel(c_ref, w_ref, b_ref, o_ref):
    a = _silu(c_ref[...])
    w_hi, w_lo = _split2(w_ref[0])
    o_ref[0] = _dot_x2(a, w_hi, w_lo) + b_ref[0]


def _modulation(c_all, w_mod, b_mod):
    depth, d, n = w_mod.shape
    rows = c_all.shape[0]
    tn = 1536
    return pl.pallas_call(
        _mod_kernel,
        grid=(depth, n // tn),
        in_specs=[
            pl.BlockSpec((rows, d), lambda l, j: (0, 0)),
            pl.BlockSpec((1, d, tn), lambda l, j: (l, 0, j)),
            pl.BlockSpec((1, 1, tn), lambda l, j: (l, 0, j)),
        ],
        out_specs=pl.BlockSpec((1, rows, tn), lambda l, j: (l, 0, j)),
        out_shape=jax.ShapeDtypeStruct((depth, rows, n), F32),
        compiler_params=_cparams(("arbitrary", "arbitrary")),
        name="modulation",
    )(c_all, w_mod, b_mod.reshape(depth, 1, n))


def _l0_kernel(x_ref, mod_ref, s0_ref, gpre_ref, gpost_ref, win_ref, wa2h_ref, wa2l_ref, ba_ref,
               glan_ref, lng_ref, lnb_ref, ws_ref, bs_ref, wout_ref,
               x1_ref, snew_ref, *rest, tm, ch, nlev, emit_vg):
    if emit_vg:
        vg_ref, s_scr = rest
    else:
        (s_scr,) = rest
    t = pl.program_id(1)

    @pl.when(t == 0)
    def _():
        s_scr[...] = s0_ref[0]

    x = x_ref[0]
    mod = mod_ref[0]
    h = _modulate(x, gpre_ref[...], mod[0:1], mod[1:2])
    p = _dot(h.astype(BF16), win_ref[...])
    o0 = 0
    q = p[:, o0:o0 + GLA_QK] * (GLA_DK ** -0.5)
    o0 += GLA_QK
    k = p[:, o0:o0 + GLA_QK]
    o0 += GLA_QK
    v = p[:, o0:o0 + GLA_V]
    o0 += GLA_V
    g = p[:, o0:o0 + GLA_V]
    o0 += GLA_V
    u = p[:, o0:o0 + GM_WIDTH]
    o0 += GM_WIDTH
    vgr = p[:, o0:o0 + GM_WIDTH]
    o0 += GM_WIDTH
    r = p[:, o0:o0 + RANK_PAD]

    xa = _dot_x2(r, wa2h_ref[...], wa2l_ref[...]) + ba_ref[...]
    la = -_softplus(-xa) * (1.0 / GLA_TAU)
    u = _gelu_tanh(u)
    vg = _gelu_tanh(vgr)
    mu = jnp.mean(vg, axis=-1, keepdims=True)
    vc = vg - mu
    vg = vc * lax.rsqrt(jnp.mean(vc * vc, axis=-1, keepdims=True) + EPS) * lng_ref[...] + lnb_ref[...]
    if emit_vg:
        vg_ref[0] = vg

    glan = glan_ref[...]
    bs = bs_ref[...]
    row = lax.broadcasted_iota(jnp.int32, (ch, GLA_QK), 0)
    ii = lax.broadcasted_iota(jnp.int32, (ch, ch), 0)
    jj = lax.broadcasted_iota(jnp.int32, (ch, ch), 1)
    wm = [jnp.where(ii >= jj, ws_ref[gg], 0.0).astype(BF16) for gg in range(GM_GROUPS)]
    diff = ii ^ jj
    top_bit = jnp.zeros_like(diff)
    for s in range(1, nlev):
        top_bit = top_bit + ((diff >> s) != 0).astype(jnp.int32)
    pair_level = jnp.where(jj < ii, top_bit, jnp.where(ii == jj, -1, -2))

    mixed = []
    for c in range(tm // ch):
        rs = slice(c * ch, (c + 1) * ch)
        qc, kc = q[rs], k[rs]
        b_in = la[rs]
        for lev in range(nlev):
            b_in = b_in + jnp.where(row >= (1 << lev), pltpu.roll(b_in, 1 << lev, 0), 0.0)
        qb, kb, vb = qc.astype(BF16), kc.astype(BF16), v[rs].astype(BF16)
        blk_end = b_in

        attn = [jnp.where(pair_level == -1, _dot_nt(qb[:, hh * GLA_DK:(hh + 1) * GLA_DK],
                                                    kb[:, hh * GLA_DK:(hh + 1) * GLA_DK]), 0.0)
                for hh in range(GLA_HEADS)]
        for lev in range(nlev):
            h = 1 << lev
            is_right = ((row >> lev) & 1) == 1
            w = jnp.exp(jnp.where(is_right, b_in - pltpu.roll(blk_end, h, 0), blk_end - b_in))
            blk_end = jnp.where(is_right, blk_end, pltpu.roll(blk_end, ch - h, 0))
            zz = (jnp.where(is_right, qc, kc) * w).astype(BF16)
            pair = pair_level == lev
            for hh in range(GLA_HEADS):
                zh = zz[:, hh * GLA_DK:(hh + 1) * GLA_DK]
                attn[hh] = jnp.where(pair, _dot_nt(zh, zh), attn[hh])

        q_in = (qc * jnp.exp(b_in)).astype(BF16)
        k_out = (kc * jnp.exp(blk_end - b_in)).astype(BF16)
        total = blk_end if ch == GLA_DV else jnp.concatenate([blk_end] * (GLA_DV // ch), axis=0)
        dcol = jnp.exp(total.T)

        parts = []
        for hh in range(GLA_HEADS):
            ks = slice(hh * GLA_DK, (hh + 1) * GLA_DK)
            vs = slice(hh * GLA_DV, (hh + 1) * GLA_DV)
            s_h = s_scr[hh]
            o_h = _dot(attn[hh].astype(BF16), vb[:, vs]) + _dot(q_in[:, ks], s_h.astype(BF16))
            s_scr[hh] = dcol[ks] * s_h + _dot_tn(k_out[:, ks], vb[:, vs])
            parts.append(_rms(o_h, glan[:, vs]))

        vgb = vg[rs].astype(BF16)
        for gg in range(GM_GROUPS):
            cs = slice(gg * GM_CH, (gg + 1) * GM_CH)
            parts.append(_dot(wm[gg], vgb[:, cs]) + bs[:, gg:gg + 1])
        mixed.append(jnp.concatenate(parts, axis=-1))

    @pl.when(t == pl.num_programs(1) - 1)
    def _():
        snew_ref[0] = s_scr[...]

    om = mixed[0] if len(mixed) == 1 else jnp.concatenate(mixed, axis=0)
    om = om * jnp.concatenate([_silu(g), u], axis=-1)
    y = _dot(om.astype(BF16), wout_ref[...])
    x1_ref[0] = x + mod[2:3] * _rms(y, gpost_ref[...])


def _layer0_mixer(x, mod, s0, g_pre, g_post, w_in, w_a2, b_a, gla_norm, ln_g, ln_b, w_s, b_s, w_out,
                  emit_vg):
    bsz, seq, d = x.shape
    ch = min(GM_BLOCK, seq)
    tm = min(L0_TILE, seq)
    nlev = int(math.log2(ch))
    o2 = 2 * GLA_QK
    o3 = o2 + GLA_V
    o4 = o3 + GLA_V
    o5 = o4 + GLA_RANK
    o6 = o5 + GM_WIDTH
    w_in_p = jnp.concatenate(
        [w_in[:, :o4], w_in[:, o5:], w_in[:, o4:o5], jnp.zeros((d, RANK_PAD - GLA_RANK), w_in.dtype)],
        axis=1).astype(BF16)
    w_a2_p = jnp.concatenate([w_a2, jnp.zeros((RANK_PAD - GLA_RANK, GLA_QK), w_a2.dtype)], axis=0)
    wa2_hi = w_a2_p.astype(BF16)
    wa2_lo = (w_a2_p - wa2_hi.astype(F32)).astype(BF16)
    ws_t = w_s[:, :ch, :ch]
    bs_t = jnp.transpose(b_s[:, :ch])

    full = _const_spec
    in_specs = [
        pl.BlockSpec((1, tm, d), lambda b, t: (b, t, 0)),
        pl.BlockSpec((1, 8, d), lambda b, t: (b, 0, 0)),
        pl.BlockSpec((1, GLA_HEADS, GLA_DK, GLA_DV), lambda b, t: (b, 0, 0, 0)),
        full((1, d)), full((1, d)),
        full((d, IN0_PAD)),
        full((RANK_PAD, GLA_QK)), full((RANK_PAD, GLA_QK)), full((1, GLA_QK)),
        full((1, GLA_V)), full((1, GM_WIDTH)), full((1, GM_WIDTH)),
        full((GM_GROUPS, ch, ch)), full((ch, GM_GROUPS)),
        full((GLA_V + GM_WIDTH, d)),
    ]
    out_specs = [
        pl.BlockSpec((1, tm, d), lambda b, t: (b, t, 0)),
        pl.BlockSpec((1, GLA_HEADS, GLA_DK, GLA_DV), lambda b, t: (b, 0, 0, 0)),
    ]
    out_shape = [
        jax.ShapeDtypeStruct((bsz, seq, d), F32),
        jax.ShapeDtypeStruct((bsz, GLA_HEADS, GLA_DK, GLA_DV), F32),
    ]
    if emit_vg:
        out_specs.append(pl.BlockSpec((1, tm, GM_WIDTH), lambda b, t: (b, t, 0)))
        out_shape.append(jax.ShapeDtypeStruct((bsz, seq, GM_WIDTH), F32))
    return pl.pallas_call(
        functools.partial(_l0_kernel, tm=tm, ch=ch, nlev=nlev, emit_vg=emit_vg),
        grid=(bsz, seq // tm),
        in_specs=in_specs,
        out_specs=out_specs,
        out_shape=out_shape,
        scratch_shapes=[pltpu.VMEM((GLA_HEADS, GLA_DK, GLA_DV), F32)],
        compiler_params=_cparams(("arbitrary", "arbitrary")),
        name="layer0_mixer",
    )(x, mod, s0, g_pre.reshape(1, d), g_post.reshape(1, d), w_in_p, wa2_hi, wa2_lo,
      b_a.reshape(1, GLA_QK), gla_norm.reshape(1, GLA_V), ln_g.reshape(1, GM_WIDTH),
      ln_b.reshape(1, GM_WIDTH), ws_t, bs_t, w_out.astype(BF16))


def _route(logits):
    col = lambda c: logits[:, c:c + 1]
    lg0, lg1 = col(0), col(1)
    mx = jnp.maximum(lg0, lg1)
    e0, e1 = jnp.exp(lg0 - mx), jnp.exp(lg1 - mx)
    den = e0 + e1
    p0, p1 = e0 / den, e1 / den
    grp1 = p1 > p0
    g_top = jnp.where(grp1, p1, p0)
    s = [jnp.where(grp1, col(2 + EXP_PER_GROUP + j), col(2 + j)) for j in range(EXP_PER_GROUP)]
    v1 = jnp.maximum(jnp.maximum(s[0], s[1]), jnp.maximum(s[2], s[3]))
    first, taken = [], None
    for j in range(EXP_PER_GROUP):
        hit = s[j] == v1
        if taken is None:
            first.append(hit)
            taken = hit
        else:
            first.append(hit & jnp.logical_not(taken))
            taken = taken | hit
    neg = jnp.float32(-jnp.inf)
    s2 = [jnp.where(first[j], neg, s[j]) for j in range(EXP_PER_GROUP)]
    v2 = jnp.maximum(jnp.maximum(s2[0], s2[1]), jnp.maximum(s2[2], s2[3]))
    second, taken = [], None
    for j in range(EXP_PER_GROUP):
        hit = (s2[j] == v2) & jnp.logical_not(first[j])
        if taken is None:
            second.append(hit)
            taken = hit
        else:
            second.append(hit & jnp.logical_not(taken))
            taken = taken | hit
    e2 = jnp.exp(v2 - v1)
    w1 = g_top * (1.0 / (1.0 + e2))
    w2 = g_top * (e2 / (1.0 + e2))
    comb = []
    for e in range(N_EXPERTS):
        in_grp = grp1 if e >= EXP_PER_GROUP else jnp.logical_not(grp1)
        j = e % EXP_PER_GROUP
        cw = jnp.where(first[j], w1, 0.0) + jnp.where(second[j], w2, 0.0)
        comb.append(jnp.where(in_grp, cw, 0.0))
    return comb


def _moe_body(x, mod, gpre, gpost, wrh, wrl, br, w1, w3, w2):
    xm = _modulate(x, gpre, mod[3:4], mod[4:5])
    logits = _dot_x2(xm, wrh, wrl) + br
    comb = _route(logits)
    xb = xm.astype(BF16)
    h1 = _dot(xb, w1)
    h3 = _dot(xb, w3)
    hid = _silu(h1) * h3
    hid = jnp.concatenate(
        [hid[:, e * D_EXPERT:(e + 1) * D_EXPERT] * comb[e] for e in range(N_EXPERTS)], axis=-1)
    f = _dot(hid.astype(BF16), w2)
    return x + mod[5:6] * _rms(f, gpost)


def _moe_kernel(x_ref, mod_ref, gpre_ref, gpost_ref, wrh_ref, wrl_ref, br_ref, w1_ref, w3_ref, w2_ref,
                o_ref):
    o_ref[0] = _moe_body(x_ref[0], mod_ref[0], gpre_ref[...], gpost_ref[...], wrh_ref[...],
                         wrl_ref[...], br_ref[...], w1_ref[...], w3_ref[...], w2_ref[...])


def _moe_weights(w_grp, b_grp, w_rt, b_rt, w1, w3, w2):
    d = w_grp.shape[0]
    pad = ROUTER_PAD - N_GROUPS - N_EXPERTS
    wr = jnp.concatenate([w_grp, w_rt, jnp.zeros((d, pad), F32)], axis=1)
    wr_hi = wr.astype(BF16)
    wr_lo = (wr - wr_hi.astype(F32)).astype(BF16)
    br = jnp.concatenate([b_grp, b_rt, jnp.zeros((pad,), F32)]).reshape(1, ROUTER_PAD)
    w1c = jnp.transpose(w1, (1, 0, 2)).reshape(d, N_EXPERTS * D_EXPERT).astype(BF16)
    w3c = jnp.transpose(w3, (1, 0, 2)).reshape(d, N_EXPERTS * D_EXPERT).astype(BF16)
    w2c = w2.reshape(N_EXPERTS * D_EXPERT, d).astype(BF16)
    return wr_hi, wr_lo, br, w1c, w3c, w2c


def _moe_specs(d):
    full = _const_spec
    hid = N_EXPERTS * D_EXPERT
    return [full((1, d)), full((1, d)), full((d, ROUTER_PAD)), full((d, ROUTER_PAD)),
            full((1, ROUTER_PAD)), full((d, hid)), full((d, hid)), full((hid, d))]


def _moe_layer(x, mod, g_pre, g_post, moe_w):
    bsz, seq, d = x.shape
    tm = min(ROW_TILE, seq)
    return pl.pallas_call(
        _moe_kernel,
        grid=(bsz, seq // tm),
        in_specs=[pl.BlockSpec((1, tm, d), lambda b, t: (b, t, 0)),
                  pl.BlockSpec((1, 8, d), lambda b, t: (b, 0, 0))] + _moe_specs(d),
        out_specs=pl.BlockSpec((1, tm, d), lambda b, t: (b, t, 0)),
        out_shape=jax.ShapeDtypeStruct((bsz, seq, d), F32),
        compiler_params=_cparams(("arbitrary", "arbitrary")),
        name="moe",
    )(x, mod, g_pre.reshape(1, d), g_post.reshape(1, d), *moe_w)


def _attn_moe_kernel(a_ref, x_ref, mod_ref, gmix_ref, wo_ref, gpre_ref, gpost_ref, wrh_ref, wrl_ref,
                     br_ref, w1_ref, w3_ref, w2_ref, o_ref):
    mod = mod_ref[0]
    y = _dot(a_ref[0], wo_ref[...])
    x1 = x_ref[0] + mod[2:3] * _rms(y, gmix_ref[...])
    o_ref[0] = _moe_body(x1, mod, gpre_ref[...], gpost_ref[...], wrh_ref[...], wrl_ref[...],
                         br_ref[...], w1_ref[...], w3_ref[...], w2_ref[...])


def _attn_out_moe_layer(attn, x, mod, g_mix_post, w_out, g_pre, g_post, moe_w):
    bsz, seq, d = x.shape
    tm = min(ROW_TILE, seq)
    tile = pl.BlockSpec((1, tm, d), lambda b, t: (b, t, 0))
    return pl.pallas_call(
        _attn_moe_kernel,
        grid=(bsz, seq // tm),
        in_specs=[tile, tile, pl.BlockSpec((1, 8, d), lambda b, t: (b, 0, 0)),
                  _const_spec((1, d)), _const_spec((d, d))] + _moe_specs(d),
        out_specs=tile,
        out_shape=jax.ShapeDtypeStruct((bsz, seq, d), F32),
        compiler_params=_cparams(("arbitrary", "arbitrary")),
        name="attn_out_moe",
    )(attn, x, mod, g_mix_post.reshape(1, d), w_out.astype(BF16), g_pre.reshape(1, d),
      g_post.reshape(1, d), *moe_w)


def _qkv_kernel(x_ref, mod_ref, gpre_ref, w_ref, k_ref, v_ref, qb_ref, kb_ref, vb_ref):
    d = x_ref.shape[-1]
    mod = mod_ref[0]
    h = _modulate(x_ref[0], gpre_ref[...], mod[0:1], mod[1:2])
    p = _dot(h.astype(BF16), w_ref[...])
    k = p[:, d:2 * d]
    v = p[:, 2 * d:3 * d]
    k_ref[0] = k
    v_ref[0] = v
    qb_ref[0] = (p[:, 0:d] * (SB_DH ** -0.5)).astype(BF16)
    kb_ref[0] = k.astype(BF16)
    vb_ref[0] = v.astype(BF16)


def _qkv_proj(x, mod, g_pre, w_qkv):
    bsz, seq, d = x.shape
    tm = min(ROW_TILE, seq)
    tile = pl.BlockSpec((1, tm, d), lambda b, t: (b, t, 0))
    return pl.pallas_call(
        _qkv_kernel,
        grid=(bsz, seq // tm),
        in_specs=[tile, pl.BlockSpec((1, 8, d), lambda b, t: (b, 0, 0)),
                  _const_spec((1, d)), _const_spec((d, 3 * d))],
        out_specs=[tile] * 5,
        out_shape=[jax.ShapeDtypeStruct((bsz, seq, d), F32)] * 2
        + [jax.ShapeDtypeStruct((bsz, seq, d), BF16)] * 3,
        compiler_params=_cparams(("arbitrary", "arbitrary")),
        name="qkv_proj",
    )(x, mod, g_pre.reshape(1, d), w_qkv.astype(BF16))


def _suffix_matrix(n):
    u = np.triu(np.ones((n, n), np.float32)).T
    return jnp.asarray(np.concatenate([u, u], axis=0), BF16)


def _stack_heads(q, first):
    zero = jnp.zeros_like(q)
    return jnp.concatenate([jnp.where(first, q, zero), jnp.where(first, zero, q)], axis=0)


def _sb_tile(qs, k, v, u2, c, vis, feature_major=False):
    z = _dot(qs, k) if feature_major else _dot_nt(qs, k)
    sp = jnp.maximum(z, jnp.log(1.0 + jnp.exp(jnp.minimum(z, SB_ZCAP))))
    if vis is not None:
        sp = jnp.where(vis, sp, 0.0)
    hi, lo = _split2(sp)
    w = _dot(jnp.concatenate([hi, lo], axis=-1), u2)
    a = jnp.exp(z - w - c)
    if vis is not None:
        a = jnp.where(vis, a, 0.0)
    ab = a.astype(BF16)
    return (_dot_nt(ab, v) if feature_major else _dot(ab, v)), c + w[:, 0:1]


def _sb_rest(tile_at, n_rest, o, c):
    def cond(s):
        return (s[0] < n_rest) & (s[1] <= SB_EXIT)

    def body(s):
        n, _, o, c = s
        d, c = tile_at(n, c)
        return n + 1, jnp.min(c), o + d, c

    s = lax.while_loop(cond, body, (jnp.int32(0), jnp.min(c), o, c))
    return s[2]


def _sb_prompt_kernel(q_ref, k_ref, v_ref, u2_ref, o_ref, *, tq):
    nq = q_ref.shape[1] // tq
    u2 = u2_ref[...]
    lane = lax.broadcasted_iota(jnp.int32, (tq, LANES), 1)
    first = lane < SB_DH
    ti = lax.broadcasted_iota(jnp.int32, (2 * tq, tq), 0)
    si = lax.broadcasted_iota(jnp.int32, (2 * tq, tq), 1)
    vis = si < (ti & (tq - 1))
    zero = jnp.zeros((2 * tq, 1), F32)

    def rows(ref, blk):
        return ref[0, pl.ds(pl.multiple_of(blk * tq, tq), tq), :]

    def store(qi, o):
        o_ref[0, pl.ds(pl.multiple_of(qi * tq, tq), tq), :] = jnp.where(first, o[:tq], o[tq:]).astype(o_ref.dtype)

    def front(qi):
        qs = _stack_heads(rows(q_ref, qi), first)
        o, c = _sb_tile(qs, rows(k_ref, qi), rows(v_ref, qi), u2, zero, vis)
        d, c = _sb_tile(qs, rows(k_ref, qi - 1), rows(v_ref, qi - 1), u2, c, None)
        return qs, o + d, c

    def finish(qi, qs, o, c):
        def tile_at(n, c):
            return _sb_tile(qs, rows(k_ref, qi - 2 - n), rows(v_ref, qi - 2 - n), u2, c, None)

        store(qi, _sb_rest(tile_at, qi - 1, o, c))

    o, _ = _sb_tile(_stack_heads(rows(q_ref, 0), first), rows(k_ref, 0), rows(v_ref, 0), u2, zero, vis)
    store(0, o)
    if nq > 1:
        store(1, front(1)[1])

    def qpair(j, _):
        fa = front(2 * j)
        fb = front(2 * j + 1)
        finish(2 * j, *fa)
        finish(2 * j + 1, *fb)
        return 0

    lax.fori_loop(1, nq // 2, qpair, 0)


def _sb_prompt(qb, kb, vb):
    bsz, seq, d = qb.shape
    tq = min(SB_TK, seq)
    assert seq == tq or (seq // tq) % 2 == 0, "query blocks are walked in pairs"
    blk = pl.BlockSpec((1, seq, LANES), lambda b, h: (b, 0, h))
    return pl.pallas_call(
        functools.partial(_sb_prompt_kernel, tq=tq),
        grid=(bsz, d // LANES),
        in_specs=[blk, blk, blk, pl.BlockSpec((2 * tq, tq), lambda b, h: (0, 0))],
        out_specs=blk,
        out_shape=jax.ShapeDtypeStruct((bsz, seq, d), BF16),
        compiler_params=_cparams(("arbitrary", "arbitrary")),
        name="sb_prompt",
    )(qb, kb, vb, _suffix_matrix(tq))


def _sb_sample_kernel(q_ref, kn_ref, vn_ref, ckl_ref, cvl_ref, ck_hbm, cv_hbm, un_ref, uc_ref, o_ref,
                      kbuf, vbuf, sem, *, tk, nblk):
    b = pl.program_id(0)
    tq = q_ref.shape[1]
    lane = lax.broadcasted_iota(jnp.int32, (tq, LANES), 1)
    first = lane < SB_DH
    ti = lax.broadcasted_iota(jnp.int32, (2 * tq, tq), 0)
    si = lax.broadcasted_iota(jnp.int32, (2 * tq, tq), 1)
    vis = si < (ti & (tq - 1))
    zero = jnp.zeros((2 * tq, 1), F32)
    un = un_ref[...]
    uc = uc_ref[...]

    def head_pair(blk):
        return blk.reshape(LANES, tk).astype(BF16)

    def pair_body(hp, _):
        c0 = pl.multiple_of(hp * LANES, LANES)
        qs = _stack_heads(q_ref[0, :, pl.ds(c0, LANES)], first)
        o, c = _sb_tile(qs, kn_ref[0, :, pl.ds(c0, LANES)], vn_ref[0, :, pl.ds(c0, LANES)], un, zero, vis)
        d, c = _sb_tile(qs, head_pair(ckl_ref[0, pl.ds(2 * hp, 2)]), head_pair(cvl_ref[0, pl.ds(2 * hp, 2)]),
                        uc, c, None, feature_major=True)

        def older_tile(n, c):
            k0 = pl.multiple_of((nblk - 2 - n) * tk, tk)
            ck = pltpu.make_async_copy(ck_hbm.at[b, pl.ds(2 * hp, 2), :, pl.ds(k0, tk)], kbuf, sem.at[0])
            cv = pltpu.make_async_copy(cv_hbm.at[b, pl.ds(2 * hp, 2), :, pl.ds(k0, tk)], vbuf, sem.at[1])
            ck.start()
            cv.start()
            ck.wait()
            cv.wait()
            return _sb_tile(qs, head_pair(kbuf[...]), head_pair(vbuf[...]), uc, c, None, feature_major=True)

        o = _sb_rest(older_tile, nblk - 1, o + d, c)
        o_ref[0, :, pl.ds(c0, LANES)] = jnp.where(first, o[:tq], o[tq:]).astype(o_ref.dtype)
        return 0

    lax.fori_loop(0, q_ref.shape[2] // LANES, pair_body, 0)


def _sb_sample(qb, kb, vb, cache_k, cache_v):
    bsz, seq, d = qb.shape
    past = cache_k.shape[3]
    tk = min(SB_TK, past)
    nblk = past // tk
    blk = pl.BlockSpec((1, seq, d), lambda b: (b, 0, 0))
    last = pl.BlockSpec((1, SB_HEADS, SB_DH, tk), lambda b: (b, 0, 0, nblk - 1))
    const = lambda shape: pl.BlockSpec(shape, lambda b: (0,) * len(shape))
    return pl.pallas_call(
        functools.partial(_sb_sample_kernel, tk=tk, nblk=nblk),
        grid=(bsz,),
        in_specs=[blk, blk, blk, last, last,
                  pl.BlockSpec(memory_space=pl.ANY), pl.BlockSpec(memory_space=pl.ANY),
                  const((2 * seq, seq)), const((2 * tk, tk))],
        out_specs=blk,
        out_shape=jax.ShapeDtypeStruct((bsz, seq, d), BF16),
        scratch_shapes=[pltpu.VMEM((2, SB_DH, tk), F32), pltpu.VMEM((2, SB_DH, tk), F32),
                        pltpu.SemaphoreType.DMA((2,))],
        compiler_params=_cparams(("arbitrary",)),
        name="sb_sample",
    )(qb, kb, vb, cache_k, cache_v, cache_k, cache_v, _suffix_matrix(seq), _suffix_matrix(tk))


def kernel(x_prompt, x_sample, c_prompt, c_sample, state_gla, cache_k, cache_v, w_mod, b_mod, g_mix_pre, g_mix_post, g_ffn_pre, g_ffn_post, w_in0, w_a2, b_a, gla_norm, gm_ln_g, gm_ln_b, w_s, b_s, w_out0, w_qkv1, w_out1, w_grp, b_grp, w_rt, b_rt, w1, w3, w2):
    bp, dec_b = x_prompt.shape[0], x_sample.shape[0]
    d = x_prompt.shape[-1]
    depth = w_mod.shape[0]

    m = _modulation(jnp.concatenate([c_prompt, c_sample], axis=0), w_mod, b_mod)
    m = m.reshape(depth, bp + dec_b, 6, d)
    m = jnp.concatenate([m, jnp.zeros((depth, bp + dec_b, 2, d), F32)], axis=2)

    xp, xs = x_prompt, x_sample
    gla_p, gla_s, gmv_s = [], [], []
    kp_l, vp_l, ks_l, vs_l = [], [], [], []
    for layer in range(depth):
        mp, ms = m[layer, :bp], m[layer, bp:]
        moe_w = _moe_weights(w_grp[layer], b_grp[layer], w_rt[layer], b_rt[layer], w1[layer], w3[layer], w2[layer])
        i = layer // 2
        if layer % 2 == 0:
            ew = (w_in0[i], w_a2[i], b_a[i], gla_norm[i], gm_ln_g[i], gm_ln_b[i], w_s[i], b_s[i], w_out0[i])
            s0 = jnp.zeros((bp, GLA_HEADS, GLA_DK, GLA_DV), state_gla.dtype)
            xp, sp = _layer0_mixer(xp, mp, s0, g_mix_pre[layer], g_mix_post[layer], *ew, emit_vg=False)
            xs, ss, vrows = _layer0_mixer(xs, ms, state_gla[i], g_mix_pre[layer], g_mix_post[layer], *ew,
                                          emit_vg=True)
            gla_p.append(sp)
            gla_s.append(ss)
            gmv_s.append(vrows)
            xp = _moe_layer(xp, mp, g_ffn_pre[layer], g_ffn_post[layer], moe_w)
            xs = _moe_layer(xs, ms, g_ffn_pre[layer], g_ffn_post[layer], moe_w)
        else:
            kp, vp, qpb, kpb, vpb = _qkv_proj(xp, mp, g_mix_pre[layer], w_qkv1[i])
            ap = _sb_prompt(qpb, kpb, vpb)
            ks, vs, qsb, ksb, vsb = _qkv_proj(xs, ms, g_mix_pre[layer], w_qkv1[i])
            past = cache_k.shape[2]
            a_s = _sb_sample(qsb, ksb, vsb, jnp.transpose(cache_k[i], (0, 2, 3, 1)),
                             jnp.transpose(cache_v[i], (0, 2, 3, 1)))
            hshape = (SB_HEADS, SB_DH)
            kp_l.append(kp.reshape(kp.shape[:2] + hshape))
            vp_l.append(vp.reshape(vp.shape[:2] + hshape))
            ks_l.append(ks.reshape(ks.shape[:2] + hshape))
            vs_l.append(vs.reshape(vs.shape[:2] + hshape))
            xp = _attn_out_moe_layer(ap, xp, mp, g_mix_post[layer], w_out1[i], g_ffn_pre[layer],
                                     g_ffn_post[layer], moe_w)
            xs = _attn_out_moe_layer(a_s, xs, ms, g_mix_post[layer], w_out1[i], g_ffn_pre[layer],
                                     g_ffn_post[layer], moe_w)
    return (xp, xs, jnp.stack(gla_p), jnp.stack(gla_s), jnp.stack(gmv_s),
            jnp.stack(kp_l), jnp.stack(vp_l), jnp.stack(ks_l), jnp.stack(vs_l))
```

```python
import functools
import math

import numpy as np
import jax
import jax.numpy as jnp
from jax import lax
from jax.experimental import pallas as pl
from jax.experimental.pallas import tpu as pltpu

F32 = jnp.float32
BF16 = jnp.bfloat16

EPS = 1e-6
D_MODEL = 1024
GLA_HEADS = 4
GLA_DK = 64
GLA_DV = 128
GLA_RANK = 16
GLA_TAU = 16.0
GLA_QK = GLA_HEADS * GLA_DK
GLA_V = GLA_HEADS * GLA_DV
GM_GROUPS = 4
GM_CH = 128
GM_WIDTH = GM_GROUPS * GM_CH
GM_BLOCK = 128
SB_HEADS = 16
SB_DH = 64
N_GROUPS = 2
EXP_PER_GROUP = 4
N_EXPERTS = 8
D_EXPERT = 256

LANES = 128
RANK_PAD = LANES
IN0_PAD = 2 * GLA_QK + 2 * GLA_V + 2 * GM_WIDTH + RANK_PAD
ROUTER_PAD = LANES
SB_TK = 256
MOE_SUB = 512
ROW_TILE = 512
L0_TILE = 512
SB_EXIT = 104.0
SB_ZCAP = 30.0
VMEM_LIMIT = 56 * 1024 * 1024


def _cparams(sem):
    return pltpu.CompilerParams(dimension_semantics=sem, vmem_limit_bytes=VMEM_LIMIT)


def _const_spec(shape):
    return pl.BlockSpec(shape, lambda b, t: (0,) * len(shape), pipeline_mode=pl.Buffered(1))


def _dot(a, b):
    return jnp.dot(a, b, preferred_element_type=F32)


def _dot_nt(a, b):
    return lax.dot_general(a, b, (((1,), (1,)), ((), ())), preferred_element_type=F32)


def _dot_tn(a, b):
    return lax.dot_general(a, b, (((0,), (0,)), ((), ())), preferred_element_type=F32)


def _split2(a):
    hi = a.astype(BF16)
    lo = (a - hi.astype(F32)).astype(BF16)
    return hi, lo


def _split3(a):
    hi = a.astype(BF16)
    r1 = a - hi.astype(F32)
    mid = r1.astype(BF16)
    lo = (r1 - mid.astype(F32)).astype(BF16)
    return hi, mid, lo


def _dot_x2(a, b_hi, b_lo):
    a_hi, a_lo = _split2(a)
    return _dot(a_hi, b_hi) + (_dot(a_lo, b_hi) + _dot(a_hi, b_lo))


def _sigmoid(x):
    return 1.0 / (1.0 + jnp.exp(-x))


def _silu(x):
    return x * _sigmoid(x)


def _softplus(x):
    return jnp.maximum(x, 0.0) + jnp.log(1.0 + jnp.exp(-jnp.abs(x)))


def _gelu_tanh(x):
    c = math.sqrt(2.0 / math.pi)
    return 0.5 * x * (1.0 + jnp.tanh(c * (x + 0.044715 * (x * x * x))))


def _rms(x, gain):
    ms = jnp.mean(x * x, axis=-1, keepdims=True)
    return x * lax.rsqrt(ms + EPS) * gain


def _modulate(x, gain, shift, scale):
    return _rms(x, gain) * (1.0 + scale) + shift


def _mod_kernel(c_ref, w_ref, b_ref, o_ref):
    a = _silu(c_ref[...])
    w_hi, w_lo = _split2(w_ref[0])
    o_ref[0] = _dot_x2(a, w_hi, w_lo) + b_ref[0]


def _modulation(c_all, w_mod, b_mod):
    depth, d, n = w_mod.shape
    rows = c_all.shape[0]
    tn = 1536
    return pl.pallas_call(
        _mod_kernel,
        grid=(depth, n // tn),
        in_specs=[
            pl.BlockSpec((rows, d), lambda l, j: (0, 0)),
            pl.BlockSpec((1, d, tn), lambda l, j: (l, 0, j)),
            pl.BlockSpec((1, 1, tn), lambda l, j: (l, 0, j)),
        ],
        out_specs=pl.BlockSpec((1, rows, tn), lambda l, j: (l, 0, j)),
        out_shape=jax.ShapeDtypeStruct((depth, rows, n), F32),
        compiler_params=_cparams(("arbitrary", "arbitrary")),
        name="modulation",
    )(c_all, w_mod, b_mod.reshape(depth, 1, n))


def _l0_kernel(x_ref, mod_ref, s0_ref, gpre_ref, gpost_ref, win_ref, wa2h_ref, wa2l_ref, ba_ref,
               glan_ref, lng_ref, lnb_ref, ws_ref, bs_ref, wout_ref,
               x1_ref, snew_ref, *rest, tm, ch, nlev, emit_vg):
    if emit_vg:
        vg_ref, s_scr = rest
    else:
        (s_scr,) = rest
    t = pl.program_id(1)

    @pl.when(t == 0)
    def _():
        s_scr[...] = s0_ref[0]

    x = x_ref[0]
    mod = mod_ref[0]
    h = _modulate(x, gpre_ref[...], mod[0:1], mod[1:2])
    p = _dot(h.astype(BF16), win_ref[...])
    o0 = 0
    q = p[:, o0:o0 + GLA_QK] * (GLA_DK ** -0.5)
    o0 += GLA_QK
    k = p[:, o0:o0 + GLA_QK]
    o0 += GLA_QK
    v = p[:, o0:o0 + GLA_V]
    o0 += GLA_V
    g = p[:, o0:o0 + GLA_V]
    o0 += GLA_V
    u = p[:, o0:o0 + GM_WIDTH]
    o0 += GM_WIDTH
    vgr = p[:, o0:o0 + GM_WIDTH]
    o0 += GM_WIDTH
    r = p[:, o0:o0 + RANK_PAD]

    xa = _dot_x2(r, wa2h_ref[...], wa2l_ref[...]) + ba_ref[...]
    la = -_softplus(-xa) * (1.0 / GLA_TAU)
    u = _gelu_tanh(u)
    vg = _gelu_tanh(vgr)
    mu = jnp.mean(vg, axis=-1, keepdims=True)
    vc = vg - mu
    vg = vc * lax.rsqrt(jnp.mean(vc * vc, axis=-1, keepdims=True) + EPS) * lng_ref[...] + lnb_ref[...]
    if emit_vg:
        vg_ref[0] = vg

    glan = glan_ref[...]
    bs = bs_ref[...]
    row = lax.broadcasted_iota(jnp.int32, (ch, GLA_QK), 0)
    ii = lax.broadcasted_iota(jnp.int32, (ch, ch), 0)
    jj = lax.broadcasted_iota(jnp.int32, (ch, ch), 1)
    wm = [jnp.where(ii >= jj, ws_ref[gg], 0.0).astype(BF16) for gg in range(GM_GROUPS)]
    diff = ii ^ jj
    top_bit = jnp.zeros_like(diff)
    for s in range(1, nlev):
        top_bit = top_bit + ((diff >> s) != 0).astype(jnp.int32)
    pair_level = jnp.where(jj < ii, top_bit, jnp.where(ii == jj, -1, -2))

    mixed = []
    for c in range(tm // ch):
        rs = slice(c * ch, (c + 1) * ch)
        qc, kc = q[rs], k[rs]
        b_in = la[rs]
        for lev in range(nlev):
            b_in = b_in + jnp.where(row >= (1 << lev), pltpu.roll(b_in, 1 << lev, 0), 0.0)
        qb, kb, vb = qc.astype(BF16), kc.astype(BF16), v[rs].astype(BF16)
        blk_end = b_in

        attn = [jnp.where(pair_level == -1, _dot_nt(qb[:, hh * GLA_DK:(hh + 1) * GLA_DK],
                                                    kb[:, hh * GLA_DK:(hh + 1) * GLA_DK]), 0.0)
                for hh in range(GLA_HEADS)]
        for lev in range(nlev):
            h = 1 << lev
            is_right = ((row >> lev) & 1) == 1
            w = jnp.exp(jnp.where(is_right, b_in - pltpu.roll(blk_end, h, 0), blk_end - b_in))
            blk_end = jnp.where(is_right, blk_end, pltpu.roll(blk_end, ch - h, 0))
            zz = (jnp.where(is_right, qc, kc) * w).astype(BF16)
            pair = pair_level == lev
            for hh in range(GLA_HEADS):
                zh = zz[:, hh * GLA_DK:(hh + 1) * GLA_DK]
                attn[hh] = jnp.where(pair, _dot_nt(zh, zh), attn[hh])

        q_in = (qc * jnp.exp(b_in)).astype(BF16)
        k_out = (kc * jnp.exp(blk_end - b_in)).astype(BF16)
        total = blk_end if ch == GLA_DV else jnp.concatenate([blk_end] * (GLA_DV // ch), axis=0)
        dcol = jnp.exp(total.T)

        parts = []
        for hh in range(GLA_HEADS):
            ks = slice(hh * GLA_DK, (hh + 1) * GLA_DK)
            vs = slice(hh * GLA_DV, (hh + 1) * GLA_DV)
            s_h = s_scr[hh]
            o_h = _dot(attn[hh].astype(BF16), vb[:, vs]) + _dot(q_in[:, ks], s_h.astype(BF16))
            s_scr[hh] = dcol[ks] * s_h + _dot_tn(k_out[:, ks], vb[:, vs])
            parts.append(_rms(o_h, glan[:, vs]))

        vgb = vg[rs].astype(BF16)
        for gg in range(GM_GROUPS):
            cs = slice(gg * GM_CH, (gg + 1) * GM_CH)
            parts.append(_dot(wm[gg], vgb[:, cs]) + bs[:, gg:gg + 1])
        mixed.append(jnp.concatenate(parts, axis=-1))

    @pl.when(t == pl.num_programs(1) - 1)
    def _():
        snew_ref[0] = s_scr[...]

    om = mixed[0] if len(mixed) == 1 else jnp.concatenate(mixed, axis=0)
    om = om * jnp.concatenate([_silu(g), u], axis=-1)
    y = _dot(om.astype(BF16), wout_ref[...])
    x1_ref[0] = x + mod[2:3] * _rms(y, gpost_ref[...])


def _layer0_mixer(x, mod, s0, g_pre, g_post, w_in, w_a2, b_a, gla_norm, ln_g, ln_b, w_s, b_s, w_out,
                  emit_vg):
    bsz, seq, d = x.shape
    ch = min(GM_BLOCK, seq)
    tm = min(L0_TILE, seq)
    nlev = int(math.log2(ch))
    o2 = 2 * GLA_QK
    o3 = o2 + GLA_V
    o4 = o3 + GLA_V
    o5 = o4 + GLA_RANK
    o6 = o5 + GM_WIDTH
    w_in_p = jnp.concatenate(
        [w_in[:, :o4], w_in[:, o5:], w_in[:, o4:o5], jnp.zeros((d, RANK_PAD - GLA_RANK), w_in.dtype)],
        axis=1).astype(BF16)
    w_a2_p = jnp.concatenate([w_a2, jnp.zeros((RANK_PAD - GLA_RANK, GLA_QK), w_a2.dtype)], axis=0)
    wa2_hi = w_a2_p.astype(BF16)
    wa2_lo = (w_a2_p - wa2_hi.astype(F32)).astype(BF16)
    ws_t = w_s[:, :ch, :ch]
    bs_t = jnp.transpose(b_s[:, :ch])

    full = _const_spec
    in_specs = [
        pl.BlockSpec((1, tm, d), lambda b, t: (b, t, 0)),
        pl.BlockSpec((1, 8, d), lambda b, t: (b, 0, 0)),
        pl.BlockSpec((1, GLA_HEADS, GLA_DK, GLA_DV), lambda b, t: (b, 0, 0, 0)),
        full((1, d)), full((1, d)),
        full((d, IN0_PAD)),
        full((RANK_PAD, GLA_QK)), full((RANK_PAD, GLA_QK)), full((1, GLA_QK)),
        full((1, GLA_V)), full((1, GM_WIDTH)), full((1, GM_WIDTH)),
        full((GM_GROUPS, ch, ch)), full((ch, GM_GROUPS)),
        full((GLA_V + GM_WIDTH, d)),
    ]
    out_specs = [
        pl.BlockSpec((1, tm, d), lambda b, t: (b, t, 0)),
        pl.BlockSpec((1, GLA_HEADS, GLA_DK, GLA_DV), lambda b, t: (b, 0, 0, 0)),
    ]
    out_shape = [
        jax.ShapeDtypeStruct((bsz, seq, d), F32),
        jax.ShapeDtypeStruct((bsz, GLA_HEADS, GLA_DK, GLA_DV), F32),
    ]
    if emit_vg:
        out_specs.append(pl.BlockSpec((1, tm, GM_WIDTH), lambda b, t: (b, t, 0)))
        out_shape.append(jax.ShapeDtypeStruct((bsz, seq, GM_WIDTH), F32))
    return pl.pallas_call(
        functools.partial(_l0_kernel, tm=tm, ch=ch, nlev=nlev, emit_vg=emit_vg),
        grid=(bsz, seq // tm),
        in_specs=in_specs,
        out_specs=out_specs,
        out_shape=out_shape,
        scratch_shapes=[pltpu.VMEM((GLA_HEADS, GLA_DK, GLA_DV), F32)],
        compiler_params=_cparams(("arbitrary", "arbitrary")),
        name="layer0_mixer",
    )(x, mod, s0, g_pre.reshape(1, d), g_post.reshape(1, d), w_in_p, wa2_hi, wa2_lo,
      b_a.reshape(1, GLA_QK), gla_norm.reshape(1, GLA_V), ln_g.reshape(1, GM_WIDTH),
      ln_b.reshape(1, GM_WIDTH), ws_t, bs_t, w_out.astype(BF16))


def _route(logits):
    col = lambda c: logits[:, c:c + 1]
    lg0, lg1 = col(0), col(1)
    mx = jnp.maximum(lg0, lg1)
    e0, e1 = jnp.exp(lg0 - mx), jnp.exp(lg1 - mx)
    den = e0 + e1
    p0, p1 = e0 / den, e1 / den
    grp1 = p1 > p0
    g_top = jnp.where(grp1, p1, p0)
    s = [jnp.where(grp1, col(2 + EXP_PER_GROUP + j), col(2 + j)) for j in range(EXP_PER_GROUP)]
    v1 = jnp.maximum(jnp.maximum(s[0], s[1]), jnp.maximum(s[2], s[3]))
    first, taken = [], None
    for j in range(EXP_PER_GROUP):
        hit = s[j] == v1
        if taken is None:
            first.append(hit)
            taken = hit
        else:
            first.append(hit & jnp.logical_not(taken))
            taken = taken | hit
    neg = jnp.float32(-jnp.inf)
    s2 = [jnp.where(first[j], neg, s[j]) for j in range(EXP_PER_GROUP)]
    v2 = jnp.maximum(jnp.maximum(s2[0], s2[1]), jnp.maximum(s2[2], s2[3]))
    second, taken = [], None
    for j in range(EXP_PER_GROUP):
        hit = (s2[j] == v2) & jnp.logical_not(first[j])
        if taken is None:
            second.append(hit)
            taken = hit
        else:
            second.append(hit & jnp.logical_not(taken))
            taken = taken | hit
    e2 = jnp.exp(v2 - v1)
    w1 = g_top * (1.0 / (1.0 + e2))
    w2 = g_top * (e2 / (1.0 + e2))
    comb = []
    for e in range(N_EXPERTS):
        in_grp = grp1 if e >= EXP_PER_GROUP else jnp.logical_not(grp1)
        j = e % EXP_PER_GROUP
        cw = jnp.where(first[j], w1, 0.0) + jnp.where(second[j], w2, 0.0)
        comb.append(jnp.where(in_grp, cw, 0.0))
    return comb


def _moe_stages(x, mod, gpre, gpost, wrh, wrl, br, w1, w3, w2):
    xm = _modulate(x, gpre, mod[3:4], mod[4:5])
    logits = _dot_x2(xm, wrh, wrl) + br
    comb = _route(logits)
    xb = xm.astype(BF16)
    yield None
    h1 = _dot(xb, w1)
    h3 = _dot(xb, w3)
    yield None
    hid = _silu(h1) * h3
    hid = jnp.concatenate(
        [hid[:, e * D_EXPERT:(e + 1) * D_EXPERT] * comb[e] for e in range(N_EXPERTS)], axis=-1)
    hid = hid.astype(BF16)
    yield None
    f = _dot(hid, w2)
    yield None
    yield x + mod[5:6] * _rms(f, gpost)


def _moe_body(x, *args):
    rows = x.shape[0]
    if rows % (2 * MOE_SUB) != 0:
        return list(_moe_stages(x, *args))[-1]
    outs = []
    for r0 in range(0, rows, 2 * MOE_SUB):
        ga = _moe_stages(x[r0:r0 + MOE_SUB], *args)
        gb = _moe_stages(x[r0 + MOE_SUB:r0 + 2 * MOE_SUB], *args)
        next(ga)
        next(ga)
        next(gb)
        next(ga)
        next(gb)
        next(ga)
        next(gb)
        oa = next(ga)
        next(gb)
        ob = next(gb)
        outs += [oa, ob]
    return jnp.concatenate(outs, axis=0)


def _moe_kernel(x_ref, mod_ref, gpre_ref, gpost_ref, wrh_ref, wrl_ref, br_ref, w1_ref, w3_ref, w2_ref,
                o_ref):
    o_ref[0] = _moe_body(x_ref[0], mod_ref[0], gpre_ref[...], gpost_ref[...], wrh_ref[...],
                         wrl_ref[...], br_ref[...], w1_ref[...], w3_ref[...], w2_ref[...])


def _moe_weights(w_grp, b_grp, w_rt, b_rt, w1, w3, w2):
    d = w_grp.shape[0]
    pad = ROUTER_PAD - N_GROUPS - N_EXPERTS
    wr = jnp.concatenate([w_grp, w_rt, jnp.zeros((d, pad), F32)], axis=1)
    wr_hi = wr.astype(BF16)
    wr_lo = (wr - wr_hi.astype(F32)).astype(BF16)
    br = jnp.concatenate([b_grp, b_rt, jnp.zeros((pad,), F32)]).reshape(1, ROUTER_PAD)
    w1c = jnp.transpose(w1, (1, 0, 2)).reshape(d, N_EXPERTS * D_EXPERT).astype(BF16)
    w3c = jnp.transpose(w3, (1, 0, 2)).reshape(d, N_EXPERTS * D_EXPERT).astype(BF16)
    w2c = w2.reshape(N_EXPERTS * D_EXPERT, d).astype(BF16)
    return wr_hi, wr_lo, br, w1c, w3c, w2c


def _moe_specs(d):
    full = _const_spec
    hid = N_EXPERTS * D_EXPERT
    return [full((1, d)), full((1, d)), full((d, ROUTER_PAD)), full((d, ROUTER_PAD)),
            full((1, ROUTER_PAD)), full((d, hid)), full((d, hid)), full((hid, d))]


def _moe_layer(x, mod, g_pre, g_post, moe_w):
    bsz, seq, d = x.shape
    tm = min(2 * MOE_SUB, seq)
    return pl.pallas_call(
        _moe_kernel,
        grid=(bsz, seq // tm),
        in_specs=[pl.BlockSpec((1, tm, d), lambda b, t: (b, t, 0)),
                  pl.BlockSpec((1, 8, d), lambda b, t: (b, 0, 0))] + _moe_specs(d),
        out_specs=pl.BlockSpec((1, tm, d), lambda b, t: (b, t, 0)),
        out_shape=jax.ShapeDtypeStruct((bsz, seq, d), F32),
        compiler_params=_cparams(("arbitrary", "arbitrary")),
        name="moe",
    )(x, mod, g_pre.reshape(1, d), g_post.reshape(1, d), *moe_w)


def _attn_moe_kernel(a_ref, x_ref, mod_ref, gmix_ref, wo_ref, gpre_ref, gpost_ref, wrh_ref, wrl_ref,
                     br_ref, w1_ref, w3_ref, w2_ref, o_ref):
    mod = mod_ref[0]
    y = _dot(a_ref[0], wo_ref[...])
    x1 = x_ref[0] + mod[2:3] * _rms(y, gmix_ref[...])
    o_ref[0] = _moe_body(x1, mod, gpre_ref[...], gpost_ref[...], wrh_ref[...], wrl_ref[...],
                         br_ref[...], w1_ref[...], w3_ref[...], w2_ref[...])


def _attn_out_moe_layer(attn, x, mod, g_mix_post, w_out, g_pre, g_post, moe_w):
    bsz, seq, d = x.shape
    tm = min(2 * MOE_SUB, seq)
    tile = pl.BlockSpec((1, tm, d), lambda b, t: (b, t, 0))
    return pl.pallas_call(
        _attn_moe_kernel,
        grid=(bsz, seq // tm),
        in_specs=[tile, tile, pl.BlockSpec((1, 8, d), lambda b, t: (b, 0, 0)),
                  _const_spec((1, d)), _const_spec((d, d))] + _moe_specs(d),
        out_specs=tile,
        out_shape=jax.ShapeDtypeStruct((bsz, seq, d), F32),
        compiler_params=_cparams(("arbitrary", "arbitrary")),
        name="attn_out_moe",
    )(attn, x, mod, g_mix_post.reshape(1, d), w_out.astype(BF16), g_pre.reshape(1, d),
      g_post.reshape(1, d), *moe_w)


def _qkv_kernel(x_ref, mod_ref, gpre_ref, w_ref, k_ref, v_ref, qb_ref, kb_ref, vb_ref):
    d = x_ref.shape[-1]
    mod = mod_ref[0]
    h = _modulate(x_ref[0], gpre_ref[...], mod[0:1], mod[1:2])
    p = _dot(h.astype(BF16), w_ref[...])
    k = p[:, d:2 * d]
    v = p[:, 2 * d:3 * d]
    k_ref[0] = k
    v_ref[0] = v
    qb_ref[0] = (p[:, 0:d] * (SB_DH ** -0.5)).astype(BF16)
    kb_ref[0] = k.astype(BF16)
    vb_ref[0] = v.astype(BF16)


def _qkv_proj(x, mod, g_pre, w_qkv):
    bsz, seq, d = x.shape
    tm = min(ROW_TILE, seq)
    tile = pl.BlockSpec((1, tm, d), lambda b, t: (b, t, 0))
    return pl.pallas_call(
        _qkv_kernel,
        grid=(bsz, seq // tm),
        in_specs=[tile, pl.BlockSpec((1, 8, d), lambda b, t: (b, 0, 0)),
                  _const_spec((1, d)), _const_spec((d, 3 * d))],
        out_specs=[tile] * 5,
        out_shape=[jax.ShapeDtypeStruct((bsz, seq, d), F32)] * 2
        + [jax.ShapeDtypeStruct((bsz, seq, d), BF16)] * 3,
        compiler_params=_cparams(("arbitrary", "arbitrary")),
        name="qkv_proj",
    )(x, mod, g_pre.reshape(1, d), w_qkv.astype(BF16))


def _suffix_matrix(n):
    u = np.triu(np.ones((n, n), np.float32)).T
    return jnp.asarray(np.concatenate([u, u], axis=0), BF16)


def _stack_heads(q, first):
    zero = jnp.zeros_like(q)
    return jnp.concatenate([jnp.where(first, q, zero), jnp.where(first, zero, q)], axis=0)


def _sb_tile(qs, k, v, u2, c, vis, feature_major=False):
    z = _dot(qs, k) if feature_major else _dot_nt(qs, k)
    sp = jnp.maximum(z, jnp.log(1.0 + jnp.exp(jnp.minimum(z, SB_ZCAP))))
    if vis is not None:
        sp = jnp.where(vis, sp, 0.0)
    hi, lo = _split2(sp)
    w = _dot(jnp.concatenate([hi, lo], axis=-1), u2)
    a = jnp.exp(z - w - c)
    if vis is not None:
        a = jnp.where(vis, a, 0.0)
    ab = a.astype(BF16)
    return (_dot_nt(ab, v) if feature_major else _dot(ab, v)), c + w[:, 0:1]


def _sb_rest(tile_at, n_rest, o, c):
    def cond(s):
        return (s[0] < n_rest) & (s[1] <= SB_EXIT)

    def body(s):
        n, _, o, c = s
        d, c = tile_at(n, c)
        return n + 1, jnp.min(c), o + d, c

    s = lax.while_loop(cond, body, (jnp.int32(0), jnp.min(c), o, c))
    return s[2]


def _sb_prompt_kernel(q_ref, k_ref, v_ref, u2_ref, o_ref, *, tq):
    nq = q_ref.shape[1] // tq
    u2 = u2_ref[...]
    lane = lax.broadcasted_iota(jnp.int32, (tq, LANES), 1)
    first = lane < SB_DH
    ti = lax.broadcasted_iota(jnp.int32, (2 * tq, tq), 0)
    si = lax.broadcasted_iota(jnp.int32, (2 * tq, tq), 1)
    vis = si < (ti & (tq - 1))
    zero = jnp.zeros((2 * tq, 1), F32)

    def rows(ref, blk):
        return ref[0, pl.ds(pl.multiple_of(blk * tq, tq), tq), :]

    def store(qi, o):
        o_ref[0, pl.ds(pl.multiple_of(qi * tq, tq), tq), :] = jnp.where(first, o[:tq], o[tq:]).astype(o_ref.dtype)

    def front(qi):
        qs = _stack_heads(rows(q_ref, qi), first)
        o, c = _sb_tile(qs, rows(k_ref, qi), rows(v_ref, qi), u2, zero, vis)
        d, c = _sb_tile(qs, rows(k_ref, qi - 1), rows(v_ref, qi - 1), u2, c, None)
        return qs, o + d, c

    def finish(qi, qs, o, c):
        def tile_at(n, c):
            return _sb_tile(qs, rows(k_ref, qi - 2 - n), rows(v_ref, qi - 2 - n), u2, c, None)

        store(qi, _sb_rest(tile_at, qi - 1, o, c))

    o, _ = _sb_tile(_stack_heads(rows(q_ref, 0), first), rows(k_ref, 0), rows(v_ref, 0), u2, zero, vis)
    store(0, o)
    if nq > 1:
        store(1, front(1)[1])

    def qpair(j, _):
        fa = front(2 * j)
        fb = front(2 * j + 1)
        finish(2 * j, *fa)
        finish(2 * j + 1, *fb)
        return 0

    lax.fori_loop(1, nq // 2, qpair, 0)


def _sb_prompt(qb, kb, vb):
    bsz, seq, d = qb.shape
    tq = min(SB_TK, seq)
    assert seq == tq or (seq // tq) % 2 == 0, "query blocks are walked in pairs"
    blk = pl.BlockSpec((1, seq, LANES), lambda b, h: (b, 0, h))
    return pl.pallas_call(
        functools.partial(_sb_prompt_kernel, tq=tq),
        grid=(bsz, d // LANES),
        in_specs=[blk, blk, blk, pl.BlockSpec((2 * tq, tq), lambda b, h: (0, 0))],
        out_specs=blk,
        out_shape=jax.ShapeDtypeStruct((bsz, seq, d), BF16),
        compiler_params=_cparams(("arbitrary", "arbitrary")),
        name="sb_prompt",
    )(qb, kb, vb, _suffix_matrix(tq))


def _sb_sample_kernel(q_ref, kn_ref, vn_ref, ckl_ref, cvl_ref, ck_hbm, cv_hbm, un_ref, uc_ref, o_ref,
                      kbuf, vbuf, sem, *, tk, nblk):
    b = pl.program_id(0)
    tq = q_ref.shape[1]
    lane = lax.broadcasted_iota(jnp.int32, (tq, LANES), 1)
    first = lane < SB_DH
    ti = lax.broadcasted_iota(jnp.int32, (2 * tq, tq), 0)
    si = lax.broadcasted_iota(jnp.int32, (2 * tq, tq), 1)
    vis = si < (ti & (tq - 1))
    zero = jnp.zeros((2 * tq, 1), F32)
    un = un_ref[...]
    uc = uc_ref[...]

    def head_pair(blk):
        return blk.reshape(LANES, tk).astype(BF16)

    def pair_body(hp, _):
        c0 = pl.multiple_of(hp * LANES, LANES)
        qs = _stack_heads(q_ref[0, :, pl.ds(c0, LANES)], first)
        o, c = _sb_tile(qs, kn_ref[0, :, pl.ds(c0, LANES)], vn_ref[0, :, pl.ds(c0, LANES)], un, zero, vis)
        d, c = _sb_tile(qs, head_pair(ckl_ref[0, pl.ds(2 * hp, 2)]), head_pair(cvl_ref[0, pl.ds(2 * hp, 2)]),
                        uc, c, None, feature_major=True)

        def older_tile(n, c):
            k0 = pl.multiple_of((nblk - 2 - n) * tk, tk)
            ck = pltpu.make_async_copy(ck_hbm.at[b, pl.ds(2 * hp, 2), :, pl.ds(k0, tk)], kbuf, sem.at[0])
            cv = pltpu.make_async_copy(cv_hbm.at[b, pl.ds(2 * hp, 2), :, pl.ds(k0, tk)], vbuf, sem.at[1])
            ck.start()
            cv.start()
            ck.wait()
            cv.wait()
            return _sb_tile(qs, head_pair(kbuf[...]), head_pair(vbuf[...]), uc, c, None, feature_major=True)

        o = _sb_rest(older_tile, nblk - 1, o + d, c)
        o_ref[0, :, pl.ds(c0, LANES)] = jnp.where(first, o[:tq], o[tq:]).astype(o_ref.dtype)
        return 0

    lax.fori_loop(0, q_ref.shape[2] // LANES, pair_body, 0)


def _sb_sample(qb, kb, vb, cache_k, cache_v):
    bsz, seq, d = qb.shape
    past = cache_k.shape[3]
    tk = min(SB_TK, past)
    nblk = past // tk
    blk = pl.BlockSpec((1, seq, d), lambda b: (b, 0, 0))
    last = pl.BlockSpec((1, SB_HEADS, SB_DH, tk), lambda b: (b, 0, 0, nblk - 1))
    const = lambda shape: pl.BlockSpec(shape, lambda b: (0,) * len(shape))
    return pl.pallas_call(
        functools.partial(_sb_sample_kernel, tk=tk, nblk=nblk),
        grid=(bsz,),
        in_specs=[blk, blk, blk, last, last,
                  pl.BlockSpec(memory_space=pl.ANY), pl.BlockSpec(memory_space=pl.ANY),
                  const((2 * seq, seq)), const((2 * tk, tk))],
        out_specs=blk,
        out_shape=jax.ShapeDtypeStruct((bsz, seq, d), BF16),
        scratch_shapes=[pltpu.VMEM((2, SB_DH, tk), F32), pltpu.VMEM((2, SB_DH, tk), F32),
                        pltpu.SemaphoreType.DMA((2,))],
        compiler_params=_cparams(("arbitrary",)),
        name="sb_sample",
    )(qb, kb, vb, cache_k, cache_v, cache_k, cache_v, _suffix_matrix(seq), _suffix_matrix(tk))


def kernel(x_prompt, x_sample, c_prompt, c_sample, state_gla, cache_k, cache_v, w_mod, b_mod, g_mix_pre, g_mix_post, g_ffn_pre, g_ffn_post, w_in0, w_a2, b_a, gla_norm, gm_ln_g, gm_ln_b, w_s, b_s, w_out0, w_qkv1, w_out1, w_grp, b_grp, w_rt, b_rt, w1, w3, w2):
    bp, dec_b = x_prompt.shape[0], x_sample.shape[0]
    d = x_prompt.shape[-1]
    depth = w_mod.shape[0]

    m = _modulation(jnp.concatenate([c_prompt, c_sample], axis=0), w_mod, b_mod)
    m = m.reshape(depth, bp + dec_b, 6, d)
    m = jnp.concatenate([m, jnp.zeros((depth, bp + dec_b, 2, d), F32)], axis=2)

    xp, xs = x_prompt, x_sample
    gla_p, gla_s, gmv_s = [], [], []
    kp_l, vp_l, ks_l, vs_l = [], [], [], []
    for layer in range(depth):
        mp, ms = m[layer, :bp], m[layer, bp:]
        moe_w = _moe_weights(w_grp[layer], b_grp[layer], w_rt[layer], b_rt[layer], w1[layer], w3[layer], w2[layer])
        i = layer // 2
        if layer % 2 == 0:
            ew = (w_in0[i], w_a2[i], b_a[i], gla_norm[i], gm_ln_g[i], gm_ln_b[i], w_s[i], b_s[i], w_out0[i])
            s0 = jnp.zeros((bp, GLA_HEADS, GLA_DK, GLA_DV), state_gla.dtype)
            xp, sp = _layer0_mixer(xp, mp, s0, g_mix_pre[layer], g_mix_post[layer], *ew, emit_vg=False)
            xs, ss, vrows = _layer0_mixer(xs, ms, state_gla[i], g_mix_pre[layer], g_mix_post[layer], *ew,
                                          emit_vg=True)
            gla_p.append(sp)
            gla_s.append(ss)
            gmv_s.append(vrows)
            xp = _moe_layer(xp, mp, g_ffn_pre[layer], g_ffn_post[layer], moe_w)
            xs = _moe_layer(xs, ms, g_ffn_pre[layer], g_ffn_post[layer], moe_w)
        else:
            kp, vp, qpb, kpb, vpb = _qkv_proj(xp, mp, g_mix_pre[layer], w_qkv1[i])
            ap = _sb_prompt(qpb, kpb, vpb)
            ks, vs, qsb, ksb, vsb = _qkv_proj(xs, ms, g_mix_pre[layer], w_qkv1[i])
            past = cache_k.shape[2]
            a_s = _sb_sample(qsb, ksb, vsb, jnp.transpose(cache_k[i], (0, 2, 3, 1)),
                             jnp.transpose(cache_v[i], (0, 2, 3, 1)))
            hshape = (SB_HEADS, SB_DH)
            kp_l.append(kp.reshape(kp.shape[:2] + hshape))
            vp_l.append(vp.reshape(vp.shape[:2] + hshape))
            ks_l.append(ks.reshape(ks.shape[:2] + hshape))
            vs_l.append(vs.reshape(vs.shape[:2] + hshape))
            xp = _attn_out_moe_layer(ap, xp, mp, g_mix_post[layer], w_out1[i], g_ffn_pre[layer],
                                     g_ffn_post[layer], moe_w)
            xs = _attn_out_moe_layer(a_s, xs, ms, g_mix_post[layer], w_out1[i], g_ffn_pre[layer],
                                     g_ffn_post[layer], moe_w)
    return (xp, xs, jnp.stack(gla_p), jnp.stack(gla_s), jnp.stack(gmv_s),
            jnp.stack(kp_l), jnp.stack(vp_l), jnp.stack(ks_l), jnp.stack(vs_l))
```

```python
import functools
import math

import numpy as np
import jax
import jax.numpy as jnp
from jax import lax
from jax.experimental import pallas as pl
from jax.experimental.pallas import tpu as pltpu

F32 = jnp.float32
BF16 = jnp.bfloat16

EPS = 1e-6
D_MODEL = 1024
GLA_HEADS = 4
GLA_DK = 64
GLA_DV = 128
GLA_RANK = 16
GLA_TAU = 16.0
GLA_QK = GLA_HEADS * GLA_DK
GLA_V = GLA_HEADS * GLA_DV
GM_GROUPS = 4
GM_CH = 128
GM_WIDTH = GM_GROUPS * GM_CH
GM_BLOCK = 128
SB_HEADS = 16
SB_DH = 64
N_GROUPS = 2
EXP_PER_GROUP = 4
N_EXPERTS = 8
D_EXPERT = 256

LANES = 128
RANK_PAD = LANES
IN0_PAD = 2 * GLA_QK + 2 * GLA_V + 2 * GM_WIDTH + RANK_PAD
ROUTER_PAD = LANES
SB_TK = 256
MOE_SUB = 512
ROW_TILE = 512
L0_TILE = 512
SB_EXIT = 104.0
SB_ZCAP = 30.0
VMEM_LIMIT = 56 * 1024 * 1024


def _cparams(sem):
    return pltpu.CompilerParams(dimension_semantics=sem, vmem_limit_bytes=VMEM_LIMIT)


def _const_spec(shape):
    return pl.BlockSpec(shape, lambda b, t: (0,) * len(shape), pipeline_mode=pl.Buffered(1))


def _dot(a, b):
    return jnp.dot(a, b, preferred_element_type=F32)


def _dot_nt(a, b):
    return lax.dot_general(a, b, (((1,), (1,)), ((), ())), preferred_element_type=F32)


def _dot_tn(a, b):
    return lax.dot_general(a, b, (((0,), (0,)), ((), ())), preferred_element_type=F32)


def _split2(a):
    hi = a.astype(BF16)
    lo = (a - hi.astype(F32)).astype(BF16)
    return hi, lo


def _split3(a):
    hi = a.astype(BF16)
    r1 = a - hi.astype(F32)
    mid = r1.astype(BF16)
    lo = (r1 - mid.astype(F32)).astype(BF16)
    return hi, mid, lo


def _dot_x2(a, b_hi, b_lo):
    a_hi, a_lo = _split2(a)
    return _dot(a_hi, b_hi) + (_dot(a_lo, b_hi) + _dot(a_hi, b_lo))


def _sigmoid(x):
    return 1.0 / (1.0 + jnp.exp(-x))


def _silu(x):
    return x * _sigmoid(x)


def _softplus(x):
    return jnp.maximum(x, 0.0) + jnp.log(1.0 + jnp.exp(-jnp.abs(x)))


def _gelu_tanh(x):
    c = math.sqrt(2.0 / math.pi)
    return 0.5 * x * (1.0 + jnp.tanh(c * (x + 0.044715 * (x * x * x))))


def _rms(x, gain):
    ms = jnp.mean(x * x, axis=-1, keepdims=True)
    return x * lax.rsqrt(ms + EPS) * gain


def _modulate(x, gain, shift, scale):
    return _rms(x, gain) * (1.0 + scale) + shift


def _mod_kernel(c_ref, w_ref, b_ref, o_ref):
    a = _silu(c_ref[...])
    w_hi, w_lo = _split2(w_ref[0])
    o_ref[0] = _dot_x2(a, w_hi, w_lo) + b_ref[0]


def _modulation(c_all, w_mod, b_mod):
    depth, d, n = w_mod.shape
    rows = c_all.shape[0]
    tn = 1536
    return pl.pallas_call(
        _mod_kernel,
        grid=(depth, n // tn),
        in_specs=[
            pl.BlockSpec((rows, d), lambda l, j: (0, 0)),
            pl.BlockSpec((1, d, tn), lambda l, j: (l, 0, j)),
            pl.BlockSpec((1, 1, tn), lambda l, j: (l, 0, j)),
        ],
        out_specs=pl.BlockSpec((1, rows, tn), lambda l, j: (l, 0, j)),
        out_shape=jax.ShapeDtypeStruct((depth, rows, n), F32),
        compiler_params=_cparams(("arbitrary", "arbitrary")),
        name="modulation",
    )(c_all, w_mod, b_mod.reshape(depth, 1, n))


def _l0_kernel(x_ref, mod_ref, s0_ref, gpre_ref, gpost_ref, win_ref, wa2h_ref, wa2l_ref, ba_ref,
               glan_ref, lng_ref, lnb_ref, ws_ref, bs_ref, wout_ref,
               x1_ref, snew_ref, *rest, tm, ch, nlev, emit_vg):
    if emit_vg:
        vg_ref, s_scr = rest
    else:
        (s_scr,) = rest
    t = pl.program_id(1)

    @pl.when(t == 0)
    def _():
        s_scr[...] = s0_ref[0]

    x = x_ref[0]
    mod = mod_ref[0]
    h = _modulate(x, gpre_ref[...], mod[0:1], mod[1:2])
    p = _dot(h.astype(BF16), win_ref[...])
    o0 = 0
    q = p[:, o0:o0 + GLA_QK] * (GLA_DK ** -0.5)
    o0 += GLA_QK
    k = p[:, o0:o0 + GLA_QK]
    o0 += GLA_QK
    v = p[:, o0:o0 + GLA_V]
    o0 += GLA_V
    g = p[:, o0:o0 + GLA_V]
    o0 += GLA_V
    u = p[:, o0:o0 + GM_WIDTH]
    o0 += GM_WIDTH
    vgr = p[:, o0:o0 + GM_WIDTH]
    o0 += GM_WIDTH
    r = p[:, o0:o0 + RANK_PAD]

    xa = _dot_x2(r, wa2h_ref[...], wa2l_ref[...]) + ba_ref[...]
    la = -_softplus(-xa) * (1.0 / GLA_TAU)
    u = _gelu_tanh(u)
    vg = _gelu_tanh(vgr)
    mu = jnp.mean(vg, axis=-1, keepdims=True)
    vc = vg - mu
    vg = vc * lax.rsqrt(jnp.mean(vc * vc, axis=-1, keepdims=True) + EPS) * lng_ref[...] + lnb_ref[...]
    if emit_vg:
        vg_ref[0] = vg

    glan = glan_ref[...]
    bs = bs_ref[...]
    row = lax.broadcasted_iota(jnp.int32, (ch, GLA_QK), 0)
    ii = lax.broadcasted_iota(jnp.int32, (ch, ch), 0)
    jj = lax.broadcasted_iota(jnp.int32, (ch, ch), 1)
    wm = [jnp.where(ii >= jj, ws_ref[gg], 0.0).astype(BF16) for gg in range(GM_GROUPS)]
    diff = ii ^ jj
    top_bit = jnp.zeros_like(diff)
    for s in range(1, nlev):
        top_bit = top_bit + ((diff >> s) != 0).astype(jnp.int32)
    pair_level = jnp.where(jj < ii, top_bit, jnp.where(ii == jj, -1, -2))

    mixed = []
    for c in range(tm // ch):
        rs = slice(c * ch, (c + 1) * ch)
        qc, kc = q[rs], k[rs]
        b_in = la[rs]
        for lev in range(nlev):
            b_in = b_in + jnp.where(row >= (1 << lev), pltpu.roll(b_in, 1 << lev, 0), 0.0)
        qb, kb, vb = qc.astype(BF16), kc.astype(BF16), v[rs].astype(BF16)
        blk_end = b_in

        attn = [jnp.where(pair_level == -1, _dot_nt(qb[:, hh * GLA_DK:(hh + 1) * GLA_DK],
                                                    kb[:, hh * GLA_DK:(hh + 1) * GLA_DK]), 0.0)
                for hh in range(GLA_HEADS)]
        for lev in range(nlev):
            h = 1 << lev
            is_right = ((row >> lev) & 1) == 1
            w = jnp.exp(jnp.where(is_right, b_in - pltpu.roll(blk_end, h, 0), blk_end - b_in))
            blk_end = jnp.where(is_right, blk_end, pltpu.roll(blk_end, ch - h, 0))
            zf = jnp.where(is_right, qc, kc) * w
            zz = zf.astype(BF16)
            zt = zf.T.astype(BF16)
            pair = pair_level == lev
            for hh in range(GLA_HEADS):
                ks = slice(hh * GLA_DK, (hh + 1) * GLA_DK)
                attn[hh] = jnp.where(pair, _dot(zz[:, ks], zt[ks]), attn[hh])

        q_in = (qc * jnp.exp(b_in)).astype(BF16)
        k_out = (kc * jnp.exp(blk_end - b_in)).astype(BF16)
        total = blk_end if ch == GLA_DV else jnp.concatenate([blk_end] * (GLA_DV // ch), axis=0)
        dcol = jnp.exp(total.T)

        parts = []
        for hh in range(GLA_HEADS):
            ks = slice(hh * GLA_DK, (hh + 1) * GLA_DK)
            vs = slice(hh * GLA_DV, (hh + 1) * GLA_DV)
            s_h = s_scr[hh]
            o_h = _dot(attn[hh].astype(BF16), vb[:, vs]) + _dot(q_in[:, ks], s_h.astype(BF16))
            s_scr[hh] = dcol[ks] * s_h + _dot_tn(k_out[:, ks], vb[:, vs])
            parts.append(_rms(o_h, glan[:, vs]))

        vgb = vg[rs].astype(BF16)
        for gg in range(GM_GROUPS):
            cs = slice(gg * GM_CH, (gg + 1) * GM_CH)
            parts.append(_dot(wm[gg], vgb[:, cs]) + bs[:, gg:gg + 1])
        mixed.append(jnp.concatenate(parts, axis=-1))

    @pl.when(t == pl.num_programs(1) - 1)
    def _():
        snew_ref[0] = s_scr[...]

    om = mixed[0] if len(mixed) == 1 else jnp.concatenate(mixed, axis=0)
    om = om * jnp.concatenate([_silu(g), u], axis=-1)
    y = _dot(om.astype(BF16), wout_ref[...])
    x1_ref[0] = x + mod[2:3] * _rms(y, gpost_ref[...])


def _layer0_mixer(x, mod, s0, g_pre, g_post, w_in, w_a2, b_a, gla_norm, ln_g, ln_b, w_s, b_s, w_out,
                  emit_vg):
    bsz, seq, d = x.shape
    ch = min(GM_BLOCK, seq)
    tm = min(L0_TILE, seq)
    nlev = int(math.log2(ch))
    o2 = 2 * GLA_QK
    o3 = o2 + GLA_V
    o4 = o3 + GLA_V
    o5 = o4 + GLA_RANK
    o6 = o5 + GM_WIDTH
    w_in_p = jnp.concatenate(
        [w_in[:, :o4], w_in[:, o5:], w_in[:, o4:o5], jnp.zeros((d, RANK_PAD - GLA_RANK), w_in.dtype)],
        axis=1).astype(BF16)
    w_a2_p = jnp.concatenate([w_a2, jnp.zeros((RANK_PAD - GLA_RANK, GLA_QK), w_a2.dtype)], axis=0)
    wa2_hi = w_a2_p.astype(BF16)
    wa2_lo = (w_a2_p - wa2_hi.astype(F32)).astype(BF16)
    ws_t = w_s[:, :ch, :ch]
    bs_t = jnp.transpose(b_s[:, :ch])

    full = _const_spec
    in_specs = [
        pl.BlockSpec((1, tm, d), lambda b, t: (b, t, 0)),
        pl.BlockSpec((1, 8, d), lambda b, t: (b, 0, 0)),
        pl.BlockSpec((1, GLA_HEADS, GLA_DK, GLA_DV), lambda b, t: (b, 0, 0, 0)),
        full((1, d)), full((1, d)),
        full((d, IN0_PAD)),
        full((RANK_PAD, GLA_QK)), full((RANK_PAD, GLA_QK)), full((1, GLA_QK)),
        full((1, GLA_V)), full((1, GM_WIDTH)), full((1, GM_WIDTH)),
        full((GM_GROUPS, ch, ch)), full((ch, GM_GROUPS)),
        full((GLA_V + GM_WIDTH, d)),
    ]
    out_specs = [
        pl.BlockSpec((1, tm, d), lambda b, t: (b, t, 0)),
        pl.BlockSpec((1, GLA_HEADS, GLA_DK, GLA_DV), lambda b, t: (b, 0, 0, 0)),
    ]
    out_shape = [
        jax.ShapeDtypeStruct((bsz, seq, d), F32),
        jax.ShapeDtypeStruct((bsz, GLA_HEADS, GLA_DK, GLA_DV), F32),
    ]
    if emit_vg:
        out_specs.append(pl.BlockSpec((1, tm, GM_WIDTH), lambda b, t: (b, t, 0)))
        out_shape.append(jax.ShapeDtypeStruct((bsz, seq, GM_WIDTH), F32))
    return pl.pallas_call(
        functools.partial(_l0_kernel, tm=tm, ch=ch, nlev=nlev, emit_vg=emit_vg),
        grid=(bsz, seq // tm),
        in_specs=in_specs,
        out_specs=out_specs,
        out_shape=out_shape,
        scratch_shapes=[pltpu.VMEM((GLA_HEADS, GLA_DK, GLA_DV), F32)],
        compiler_params=_cparams(("arbitrary", "arbitrary")),
        name="layer0_mixer",
    )(x, mod, s0, g_pre.reshape(1, d), g_post.reshape(1, d), w_in_p, wa2_hi, wa2_lo,
      b_a.reshape(1, GLA_QK), gla_norm.reshape(1, GLA_V), ln_g.reshape(1, GM_WIDTH),
      ln_b.reshape(1, GM_WIDTH), ws_t, bs_t, w_out.astype(BF16))


def _route(logits):
    col = lambda c: logits[:, c:c + 1]
    lg0, lg1 = col(0), col(1)
    mx = jnp.maximum(lg0, lg1)
    e0, e1 = jnp.exp(lg0 - mx), jnp.exp(lg1 - mx)
    den = e0 + e1
    p0, p1 = e0 / den, e1 / den
    grp1 = p1 > p0
    g_top = jnp.where(grp1, p1, p0)
    s = [jnp.where(grp1, col(2 + EXP_PER_GROUP + j), col(2 + j)) for j in range(EXP_PER_GROUP)]
    v1 = jnp.maximum(jnp.maximum(s[0], s[1]), jnp.maximum(s[2], s[3]))
    first, taken = [], None
    for j in range(EXP_PER_GROUP):
        hit = s[j] == v1
        if taken is None:
            first.append(hit)
            taken = hit
        else:
            first.append(hit & jnp.logical_not(taken))
            taken = taken | hit
    neg = jnp.float32(-jnp.inf)
    s2 = [jnp.where(first[j], neg, s[j]) for j in range(EXP_PER_GROUP)]
    v2 = jnp.maximum(jnp.maximum(s2[0], s2[1]), jnp.maximum(s2[2], s2[3]))
    second, taken = [], None
    for j in range(EXP_PER_GROUP):
        hit = (s2[j] == v2) & jnp.logical_not(first[j])
        if taken is None:
            second.append(hit)
            taken = hit
        else:
            second.append(hit & jnp.logical_not(taken))
            taken = taken | hit
    e2 = jnp.exp(v2 - v1)
    w1 = g_top * (1.0 / (1.0 + e2))
    w2 = g_top * (e2 / (1.0 + e2))
    comb = []
    for e in range(N_EXPERTS):
        in_grp = grp1 if e >= EXP_PER_GROUP else jnp.logical_not(grp1)
        j = e % EXP_PER_GROUP
        cw = jnp.where(first[j], w1, 0.0) + jnp.where(second[j], w2, 0.0)
        comb.append(jnp.where(in_grp, cw, 0.0))
    return comb


def _moe_stages(x, mod, gpre, gpost, wrh, wrl, br, w1, w3, w2):
    xm = _modulate(x, gpre, mod[3:4], mod[4:5])
    logits = _dot_x2(xm, wrh, wrl) + br
    comb = _route(logits)
    xb = xm.astype(BF16)
    yield None
    h1 = _dot(xb, w1)
    h3 = _dot(xb, w3)
    yield None
    hid = _silu(h1) * h3
    hid = jnp.concatenate(
        [hid[:, e * D_EXPERT:(e + 1) * D_EXPERT] * comb[e] for e in range(N_EXPERTS)], axis=-1)
    hid = hid.astype(BF16)
    yield None
    f = _dot(hid, w2)
    yield None
    yield x + mod[5:6] * _rms(f, gpost)


def _moe_body(x, *args):
    rows = x.shape[0]
    if rows % (2 * MOE_SUB) != 0:
        return list(_moe_stages(x, *args))[-1]
    outs = []
    for r0 in range(0, rows, 2 * MOE_SUB):
        ga = _moe_stages(x[r0:r0 + MOE_SUB], *args)
        gb = _moe_stages(x[r0 + MOE_SUB:r0 + 2 * MOE_SUB], *args)
        next(ga)
        next(ga)
        next(gb)
        next(ga)
        next(gb)
        next(ga)
        next(gb)
        oa = next(ga)
        next(gb)
        ob = next(gb)
        outs += [oa, ob]
    return jnp.concatenate(outs, axis=0)


def _moe_kernel(x_ref, mod_ref, gpre_ref, gpost_ref, wrh_ref, wrl_ref, br_ref, w1_ref, w3_ref, w2_ref,
                o_ref):
    o_ref[0] = _moe_body(x_ref[0], mod_ref[0], gpre_ref[...], gpost_ref[...], wrh_ref[...],
                         wrl_ref[...], br_ref[...], w1_ref[...], w3_ref[...], w2_ref[...])


def _moe_weights(w_grp, b_grp, w_rt, b_rt, w1, w3, w2):
    d = w_grp.shape[0]
    pad = ROUTER_PAD - N_GROUPS - N_EXPERTS
    wr = jnp.concatenate([w_grp, w_rt, jnp.zeros((d, pad), F32)], axis=1)
    wr_hi = wr.astype(BF16)
    wr_lo = (wr - wr_hi.astype(F32)).astype(BF16)
    br = jnp.concatenate([b_grp, b_rt, jnp.zeros((pad,), F32)]).reshape(1, ROUTER_PAD)
    w1c = jnp.transpose(w1, (1, 0, 2)).reshape(d, N_EXPERTS * D_EXPERT).astype(BF16)
    w3c = jnp.transpose(w3, (1, 0, 2)).reshape(d, N_EXPERTS * D_EXPERT).astype(BF16)
    w2c = w2.reshape(N_EXPERTS * D_EXPERT, d).astype(BF16)
    return wr_hi, wr_lo, br, w1c, w3c, w2c


def _moe_specs(d):
    full = _const_spec
    hid = N_EXPERTS * D_EXPERT
    return [full((1, d)), full((1, d)), full((d, ROUTER_PAD)), full((d, ROUTER_PAD)),
            full((1, ROUTER_PAD)), full((d, hid)), full((d, hid)), full((hid, d))]


def _moe_layer(x, mod, g_pre, g_post, moe_w):
    bsz, seq, d = x.shape
    tm = min(2 * MOE_SUB, seq)
    return pl.pallas_call(
        _moe_kernel,
        grid=(bsz, seq // tm),
        in_specs=[pl.BlockSpec((1, tm, d), lambda b, t: (b, t, 0)),
                  pl.BlockSpec((1, 8, d), lambda b, t: (b, 0, 0))] + _moe_specs(d),
        out_specs=pl.BlockSpec((1, tm, d), lambda b, t: (b, t, 0)),
        out_shape=jax.ShapeDtypeStruct((bsz, seq, d), F32),
        compiler_params=_cparams(("arbitrary", "arbitrary")),
        name="moe",
    )(x, mod, g_pre.reshape(1, d), g_post.reshape(1, d), *moe_w)


def _attn_moe_kernel(a_ref, x_ref, mod_ref, gmix_ref, wo_ref, gpre_ref, gpost_ref, wrh_ref, wrl_ref,
                     br_ref, w1_ref, w3_ref, w2_ref, o_ref):
    mod = mod_ref[0]
    y = _dot(a_ref[0], wo_ref[...])
    x1 = x_ref[0] + mod[2:3] * _rms(y, gmix_ref[...])
    o_ref[0] = _moe_body(x1, mod, gpre_ref[...], gpost_ref[...], wrh_ref[...], wrl_ref[...],
                         br_ref[...], w1_ref[...], w3_ref[...], w2_ref[...])


def _attn_out_moe_layer(attn, x, mod, g_mix_post, w_out, g_pre, g_post, moe_w):
    bsz, seq, d = x.shape
    tm = min(2 * MOE_SUB, seq)
    tile = pl.BlockSpec((1, tm, d), lambda b, t: (b, t, 0))
    return pl.pallas_call(
        _attn_moe_kernel,
        grid=(bsz, seq // tm),
        in_specs=[tile, tile, pl.BlockSpec((1, 8, d), lambda b, t: (b, 0, 0)),
                  _const_spec((1, d)), _const_spec((d, d))] + _moe_specs(d),
        out_specs=tile,
        out_shape=jax.ShapeDtypeStruct((bsz, seq, d), F32),
        compiler_params=_cparams(("arbitrary", "arbitrary")),
        name="attn_out_moe",
    )(attn, x, mod, g_mix_post.reshape(1, d), w_out.astype(BF16), g_pre.reshape(1, d),
      g_post.reshape(1, d), *moe_w)


def _qkv_kernel(x_ref, mod_ref, gpre_ref, w_ref, k_ref, v_ref, qb_ref, kb_ref, vb_ref):
    d = x_ref.shape[-1]
    mod = mod_ref[0]
    h = _modulate(x_ref[0], gpre_ref[...], mod[0:1], mod[1:2])
    p = _dot(h.astype(BF16), w_ref[...])
    k = p[:, d:2 * d]
    v = p[:, 2 * d:3 * d]
    k_ref[0] = k
    v_ref[0] = v
    qb_ref[0] = (p[:, 0:d] * (SB_DH ** -0.5)).astype(BF16)
    kb_ref[0] = k.astype(BF16)
    vb_ref[0] = v.astype(BF16)


def _qkv_proj(x, mod, g_pre, w_qkv):
    bsz, seq, d = x.shape
    tm = min(ROW_TILE, seq)
    tile = pl.BlockSpec((1, tm, d), lambda b, t: (b, t, 0))
    return pl.pallas_call(
        _qkv_kernel,
        grid=(bsz, seq // tm),
        in_specs=[tile, pl.BlockSpec((1, 8, d), lambda b, t: (b, 0, 0)),
                  _const_spec((1, d)), _const_spec((d, 3 * d))],
        out_specs=[tile] * 5,
        out_shape=[jax.ShapeDtypeStruct((bsz, seq, d), F32)] * 2
        + [jax.ShapeDtypeStruct((bsz, seq, d), BF16)] * 3,
        compiler_params=_cparams(("arbitrary", "arbitrary")),
        name="qkv_proj",
    )(x, mod, g_pre.reshape(1, d), w_qkv.astype(BF16))


def _suffix_matrix(n):
    u = np.triu(np.ones((n, n), np.float32)).T
    return jnp.asarray(np.concatenate([u, u], axis=0), BF16)


def _stack_heads(q, first):
    zero = jnp.zeros_like(q)
    return jnp.concatenate([jnp.where(first, q, zero), jnp.where(first, zero, q)], axis=0)


def _sb_tile(qs, k, v, u2, c, vis, feature_major=False):
    z = _dot(qs, k) if feature_major else _dot_nt(qs, k)
    sp = jnp.maximum(z, jnp.log(1.0 + jnp.exp(jnp.minimum(z, SB_ZCAP))))
    if vis is not None:
        sp = jnp.where(vis, sp, 0.0)
    hi, lo = _split2(sp)
    w = _dot(jnp.concatenate([hi, lo], axis=-1), u2)
    a = jnp.exp(z - w - c)
    if vis is not None:
        a = jnp.where(vis, a, 0.0)
    ab = a.astype(BF16)
    return (_dot_nt(ab, v) if feature_major else _dot(ab, v)), c + w[:, 0:1]


def _sb_rest(tile_at, n_rest, o, c):
    def cond(s):
        return (s[0] < n_rest) & (s[1] <= SB_EXIT)

    def body(s):
        n, _, o, c = s
        d, c = tile_at(n, c)
        return n + 1, jnp.min(c), o + d, c

    s = lax.while_loop(cond, body, (jnp.int32(0), jnp.min(c), o, c))
    return s[2]


def _sb_prompt_kernel(q_ref, k_ref, v_ref, u2_ref, o_ref, *, tq):
    nq = q_ref.shape[1] // tq
    u2 = u2_ref[...]
    lane = lax.broadcasted_iota(jnp.int32, (tq, LANES), 1)
    first = lane < SB_DH
    ti = lax.broadcasted_iota(jnp.int32, (2 * tq, tq), 0)
    si = lax.broadcasted_iota(jnp.int32, (2 * tq, tq), 1)
    vis = si < (ti & (tq - 1))
    zero = jnp.zeros((2 * tq, 1), F32)

    def rows(ref, blk):
        return ref[0, pl.ds(pl.multiple_of(blk * tq, tq), tq), :]

    def store(qi, o):
        o_ref[0, pl.ds(pl.multiple_of(qi * tq, tq), tq), :] = jnp.where(first, o[:tq], o[tq:]).astype(o_ref.dtype)

    def front(qi):
        qs = _stack_heads(rows(q_ref, qi), first)
        o, c = _sb_tile(qs, rows(k_ref, qi), rows(v_ref, qi), u2, zero, vis)
        d, c = _sb_tile(qs, rows(k_ref, qi - 1), rows(v_ref, qi - 1), u2, c, None)
        return qs, o + d, c

    def finish(qi, qs, o, c):
        def tile_at(n, c):
            return _sb_tile(qs, rows(k_ref, qi - 2 - n), rows(v_ref, qi - 2 - n), u2, c, None)

        store(qi, _sb_rest(tile_at, qi - 1, o, c))

    o, _ = _sb_tile(_stack_heads(rows(q_ref, 0), first), rows(k_ref, 0), rows(v_ref, 0), u2, zero, vis)
    store(0, o)
    if nq > 1:
        store(1, front(1)[1])

    def qpair(j, _):
        fa = front(2 * j)
        fb = front(2 * j + 1)
        finish(2 * j, *fa)
        finish(2 * j + 1, *fb)
        return 0

    lax.fori_loop(1, nq // 2, qpair, 0)


def _sb_prompt(qb, kb, vb):
    bsz, seq, d = qb.shape
    tq = min(SB_TK, seq)
    assert seq == tq or (seq // tq) % 2 == 0, "query blocks are walked in pairs"
    blk = pl.BlockSpec((1, seq, LANES), lambda b, h: (b, 0, h))
    return pl.pallas_call(
        functools.partial(_sb_prompt_kernel, tq=tq),
        grid=(bsz, d // LANES),
        in_specs=[blk, blk, blk, pl.BlockSpec((2 * tq, tq), lambda b, h: (0, 0))],
        out_specs=blk,
        out_shape=jax.ShapeDtypeStruct((bsz, seq, d), BF16),
        compiler_params=_cparams(("arbitrary", "arbitrary")),
        name="sb_prompt",
    )(qb, kb, vb, _suffix_matrix(tq))


def _sb_sample_kernel(q_ref, kn_ref, vn_ref, ckl_ref, cvl_ref, ck_hbm, cv_hbm, un_ref, uc_ref, o_ref,
                      kbuf, vbuf, sem, *, tk, nblk):
    b = pl.program_id(0)
    tq = q_ref.shape[1]
    lane = lax.broadcasted_iota(jnp.int32, (tq, LANES), 1)
    first = lane < SB_DH
    ti = lax.broadcasted_iota(jnp.int32, (2 * tq, tq), 0)
    si = lax.broadcasted_iota(jnp.int32, (2 * tq, tq), 1)
    vis = si < (ti & (tq - 1))
    zero = jnp.zeros((2 * tq, 1), F32)
    un = un_ref[...]
    uc = uc_ref[...]

    def head_pair(blk):
        return blk.reshape(LANES, tk).astype(BF16)

    def pair_body(hp, _):
        c0 = pl.multiple_of(hp * LANES, LANES)
        qs = _stack_heads(q_ref[0, :, pl.ds(c0, LANES)], first)
        o, c = _sb_tile(qs, kn_ref[0, :, pl.ds(c0, LANES)], vn_ref[0, :, pl.ds(c0, LANES)], un, zero, vis)
        d, c = _sb_tile(qs, head_pair(ckl_ref[0, pl.ds(2 * hp, 2)]), head_pair(cvl_ref[0, pl.ds(2 * hp, 2)]),
                        uc, c, None, feature_major=True)

        def older_tile(n, c):
            k0 = pl.multiple_of((nblk - 2 - n) * tk, tk)
            ck = pltpu.make_async_copy(ck_hbm.at[b, pl.ds(2 * hp, 2), :, pl.ds(k0, tk)], kbuf, sem.at[0])
            cv = pltpu.make_async_copy(cv_hbm.at[b, pl.ds(2 * hp, 2), :, pl.ds(k0, tk)], vbuf, sem.at[1])
            ck.start()
            cv.start()
            ck.wait()
            cv.wait()
            return _sb_tile(qs, head_pair(kbuf[...]), head_pair(vbuf[...]), uc, c, None, feature_major=True)

        o = _sb_rest(older_tile, nblk - 1, o + d, c)
        o_ref[0, :, pl.ds(c0, LANES)] = jnp.where(first, o[:tq], o[tq:]).astype(o_ref.dtype)
        return 0

    lax.fori_loop(0, q_ref.shape[2] // LANES, pair_body, 0)


def _sb_sample(qb, kb, vb, cache_k, cache_v):
    bsz, seq, d = qb.shape
    past = cache_k.shape[3]
    tk = min(SB_TK, past)
    nblk = past // tk
    blk = pl.BlockSpec((1, seq, d), lambda b: (b, 0, 0))
    last = pl.BlockSpec((1, SB_HEADS, SB_DH, tk), lambda b: (b, 0, 0, nblk - 1))
    const = lambda shape: pl.BlockSpec(shape, lambda b: (0,) * len(shape))
    return pl.pallas_call(
        functools.partial(_sb_sample_kernel, tk=tk, nblk=nblk),
        grid=(bsz,),
        in_specs=[blk, blk, blk, last, last,
                  pl.BlockSpec(memory_space=pl.ANY), pl.BlockSpec(memory_space=pl.ANY),
                  const((2 * seq, seq)), const((2 * tk, tk))],
        out_specs=blk,
        out_shape=jax.ShapeDtypeStruct((bsz, seq, d), BF16),
        scratch_shapes=[pltpu.VMEM((2, SB_DH, tk), F32), pltpu.VMEM((2, SB_DH, tk), F32),
                        pltpu.SemaphoreType.DMA((2,))],
        compiler_params=_cparams(("arbitrary",)),
        name="sb_sample",
    )(qb, kb, vb, cache_k, cache_v, cache_k, cache_v, _suffix_matrix(seq), _suffix_matrix(tk))


def kernel(x_prompt, x_sample, c_prompt, c_sample, state_gla, cache_k, cache_v, w_mod, b_mod, g_mix_pre, g_mix_post, g_ffn_pre, g_ffn_post, w_in0, w_a2, b_a, gla_norm, gm_ln_g, gm_ln_b, w_s, b_s, w_out0, w_qkv1, w_out1, w_grp, b_grp, w_rt, b_rt, w1, w3, w2):
    bp, dec_b = x_prompt.shape[0], x_sample.shape[0]
    d = x_prompt.shape[-1]
    depth = w_mod.shape[0]

    m = _modulation(jnp.concatenate([c_prompt, c_sample], axis=0), w_mod, b_mod)
    m = m.reshape(depth, bp + dec_b, 6, d)
    m = jnp.concatenate([m, jnp.zeros((depth, bp + dec_b, 2, d), F32)], axis=2)

    xp, xs = x_prompt, x_sample
    gla_p, gla_s, gmv_s = [], [], []
    kp_l, vp_l, ks_l, vs_l = [], [], [], []
    for layer in range(depth):
        mp, ms = m[layer, :bp], m[layer, bp:]
        moe_w = _moe_weights(w_grp[layer], b_grp[layer], w_rt[layer], b_rt[layer], w1[layer], w3[layer], w2[layer])
        i = layer // 2
        if layer % 2 == 0:
            ew = (w_in0[i], w_a2[i], b_a[i], gla_norm[i], gm_ln_g[i], gm_ln_b[i], w_s[i], b_s[i], w_out0[i])
            s0 = jnp.zeros((bp, GLA_HEADS, GLA_DK, GLA_DV), state_gla.dtype)
            xp, sp = _layer0_mixer(xp, mp, s0, g_mix_pre[layer], g_mix_post[layer], *ew, emit_vg=False)
            xs, ss, vrows = _layer0_mixer(xs, ms, state_gla[i], g_mix_pre[layer], g_mix_post[layer], *ew,
                                          emit_vg=True)
            gla_p.append(sp)
            gla_s.append(ss)
            gmv_s.append(vrows)
            xp = _moe_layer(xp, mp, g_ffn_pre[layer], g_ffn_post[layer], moe_w)
            xs = _moe_layer(xs, ms, g_ffn_pre[layer], g_ffn_post[layer], moe_w)
        else:
            kp, vp, qpb, kpb, vpb = _qkv_proj(xp, mp, g_mix_pre[layer], w_qkv1[i])
            ap = _sb_prompt(qpb, kpb, vpb)
            ks, vs, qsb, ksb, vsb = _qkv_proj(xs, ms, g_mix_pre[layer], w_qkv1[i])
            past = cache_k.shape[2]
            a_s = _sb_sample(qsb, ksb, vsb, jnp.transpose(cache_k[i], (0, 2, 3, 1)),
                             jnp.transpose(cache_v[i], (0, 2, 3, 1)))
            hshape = (SB_HEADS, SB_DH)
            kp_l.append(kp.reshape(kp.shape[:2] + hshape))
            vp_l.append(vp.reshape(vp.shape[:2] + hshape))
            ks_l.append(ks.reshape(ks.shape[:2] + hshape))
            vs_l.append(vs.reshape(vs.shape[:2] + hshape))
            xp = _attn_out_moe_layer(ap, xp, mp, g_mix_post[layer], w_out1[i], g_ffn_pre[layer],
                                     g_ffn_post[layer], moe_w)
            xs = _attn_out_moe_layer(a_s, xs, ms, g_mix_post[layer], w_out1[i], g_ffn_pre[layer],
                                     g_ffn_post[layer], moe_w)
    return (xp, xs, jnp.stack(gla_p), jnp.stack(gla_s), jnp.stack(gmv_s),
            jnp.stack(kp_l), jnp.stack(vp_l), jnp.stack(ks_l), jnp.stack(vs_l))
```
